```python
import jax, jax.numpy as jnp
from jax import lax
import numpy as np

D_MODEL = 1024
BATCH = 1
SEQ = 16384
DEPTH = 4

CHUNK = 64
N_A_LAYERS = DEPTH // 2
N_B_LAYERS = DEPTH - N_A_LAYERS
HEAD_DIM = 64
N_MIX_HEADS = (3 * D_MODEL // 4) // HEAD_DIM
D_MIX = N_MIX_HEADS * HEAD_DIM
N_MEM_HEADS = 4
D_MEMQ = D_MODEL - D_MIX
N_MEM = 256
LORA_W = 64
LORA_A = 64
LORA_V = 32
LORA_G = 128
LN_X_EPS = 64e-5
RMS_EPS = 1e-6
LEFT_CHUNKS = 8
BAND = (LEFT_CHUNKS + 1) * CHUNK
PAD_LEN = LEFT_CHUNKS * CHUNK
REL_CLIP = 256
REL_TABLE = CHUNK + REL_CLIP
D_FF = 256 * ((8 * D_MODEL // 3 + 255) // 256)
CONV_W = 3

kernel_name = 'hybrid_rwkv7_chunkattn_yoco_mem'


def rms_norm(x, g):
    xf = x.astype(jnp.float32)
    y = xf * lax.rsqrt(jnp.mean(xf * xf, axis=-1, keepdims=True) + RMS_EPS)
    return (y * g.astype(jnp.float32)).astype(x.dtype)


def token_shift(x):
    return jnp.pad(x[:, :-1], ((0, 0), (1, 0), (0, 0)))


def wkv7_scan(r, w, k, v, a, b):
    bsz, _, nh, hd = r.shape
    xs = tuple(jnp.swapaxes(t, 0, 1) for t in (r, w, k, v, a, b))

    def step(state, inp):
        r_t, w_t, k_t, v_t, a_t, b_t = inp
        sa = jnp.einsum('bhvk,bhk->bhv', state, a_t)
        state = (state * w_t[:, :, None, :] + sa[..., None] * b_t[:, :, None, :]
                 + v_t[..., None] * k_t[:, :, None, :])
        return state, jnp.einsum('bhvk,bhk->bhv', state, r_t)

    s0 = jnp.zeros((bsz, nh, hd, hd), jnp.float32)
    _, ys = lax.scan(step, s0, xs)
    return jnp.swapaxes(ys, 0, 1)


def rwkv7_time_mix(h, p_rkv, mu_rkv, mu_x, w0, w1, w2, a0, a1, a2, g1, g2,
                   k_k, k_a, r_k, lnx_w, lnx_b, v_first, v_res):
    bsz, seq, _ = h.shape
    f32 = jnp.float32
    dh = token_shift(h) - h
    xw = h + dh * mu_x[0]
    xa = h + dh * mu_x[1]
    xg = h + dh * mu_x[2]
    p_r, p_k, p_v = jnp.split(p_rkv, 3, axis=-1)
    r = p_r + (token_shift(p_r) - p_r) * mu_rkv[0]
    k = p_k + (token_shift(p_k) - p_k) * mu_rkv[1]
    v = p_v + (token_shift(p_v) - p_v) * mu_rkv[2]
    if v_res is None:
        v_first = v
    else:
        mu_v, v0, v1, v2 = v_res
        xv = h + dh * mu_v
        v = v + (v_first - v) * jax.nn.sigmoid(v0 + (xv @ v1) @ v2)
    w_log = (-jax.nn.softplus(-(w0 + jnp.tanh(xw @ w1) @ w2)) - 0.5).astype(f32)
    decay = jnp.exp(-jnp.exp(w_log))
    a = jax.nn.sigmoid(a0 + (xa @ a1) @ a2).astype(f32)
    g = jax.nn.sigmoid(xg @ g1) @ g2

    def heads(t):
        return t.astype(f32).reshape(bsz, seq, N_MIX_HEADS, HEAD_DIM)

    def per_head(p):
        return p.astype(f32).reshape(N_MIX_HEADS, HEAD_DIM)

    r_h, k_h, v_h, w_h, a_h = heads(r), heads(k), heads(v), heads(decay), heads(a)
    kk = k_h * per_head(k_k)
    kk = kk / jnp.maximum(jnp.sqrt(jnp.sum(kk * kk, axis=-1, keepdims=True)), 1e-12)
    k_h = k_h * (1.0 + (a_h - 1.0) * per_head(k_a))
    y = wkv7_scan(r_h, w_h, k_h, v_h, -kk, kk * a_h)
    mean = jnp.mean(y, axis=-1, keepdims=True)
    var = jnp.mean(jnp.square(y - mean), axis=-1, keepdims=True)
    y = (y - mean) * lax.rsqrt(var + LN_X_EPS) * per_head(lnx_w) + per_head(lnx_b)
    y = y + jnp.sum(r_h * k_h * r_k.astype(f32), axis=-1, keepdims=True) * v_h
    out = y.reshape(bsz, seq, D_MIX).astype(h.dtype) * g
    return out, v_first


def rel_bias_band(table):
    i = jnp.arange(CHUNK)[:, None]
    j = jnp.arange(BAND)[None, :]
    dist = i + PAD_LEN - j
    idx = jnp.clip(dist, -(CHUNK - 1), REL_CLIP) + (CHUNK - 1)
    return table[:, idx].astype(jnp.float32)


def chunk_attention(q, k_pad, v_pad, bias):
    bsz, nh, seq, hd = q.shape
    n_chunks = seq // CHUNK
    scale = HEAD_DIM ** -0.5
    neg = jnp.finfo(jnp.float32).min

    def one_chunk(c):
        start = c * CHUNK
        qc = lax.dynamic_slice_in_dim(q, start, CHUNK, axis=2)
        kc = lax.dynamic_slice_in_dim(k_pad, start, BAND, axis=2)
        vc = lax.dynamic_slice_in_dim(v_pad, start, BAND, axis=2)
        s = jnp.einsum('bhqd,bhkd->bhqk', qc, kc).astype(jnp.float32) * scale + bias
        kpos = start - PAD_LEN + jnp.arange(BAND)
        s = jnp.where(kpos >= 0, s, neg)
        p = jax.nn.softmax(s, axis=-1).astype(vc.dtype)
        return jnp.einsum('bhqk,bhkd->bhqd', p, vc)

    o = lax.map(one_chunk, jnp.arange(n_chunks))
    return o.transpose(1, 0, 3, 2, 4).reshape(bsz, seq, nh * hd)


def memory_attention(q_mem, mem_n, w_mem_kv):
    bsz, seq, _ = q_mem.shape
    k_m, v_m = jnp.split(mem_n @ w_mem_kv, 2, axis=-1)
    q = q_mem.reshape(bsz, seq, N_MEM_HEADS, HEAD_DIM)
    k = k_m.reshape(bsz, -1, N_MEM_HEADS, HEAD_DIM)
    v = v_m.reshape(bsz, -1, N_MEM_HEADS, HEAD_DIM)
    s = jnp.einsum('bshd,bmhd->bhsm', q, k).astype(jnp.float32) * HEAD_DIM ** -0.5
    p = jax.nn.softmax(s, axis=-1).astype(v.dtype)
    return jnp.einsum('bhsm,bmhd->bshd', p, v).reshape(bsz, seq, D_MEMQ)


def conv_glu(h, w_in, conv_w, conv_b, w_out):
    seq = h.shape[1]
    gate, val = jnp.split(h @ w_in, 2, axis=-1)
    gp = jnp.pad(gate, ((0, 0), (CONV_W - 1, 0), (0, 0)))
    conv = conv_b + sum(conv_w[j] * gp[:, j:j + seq] for j in range(CONV_W))
    return (jax.nn.gelu(conv, approximate=False) * val) @ w_out


def setup_inputs(seed: int = 0) -> dict:
    key = jax.random.key(seed)
    ks = iter(jax.random.split(key, 48))
    f32 = jnp.float32
    D = D_MODEL
    NA, NB = N_A_LAYERS, N_B_LAYERS
    NV = max(NA - 1, 0)

    def nrm(shape, scale):
        return jax.random.normal(next(ks), shape, f32) * scale

    def gain(shape):
        return 1.0 + nrm(shape, 0.02)

    def unif(shape):
        return jax.random.uniform(next(ks), shape, f32)

    n = jnp.arange(D_MIX, dtype=f32) / (D_MIX - 1)
    w0_base = -6.5 + 5.0 * n ** 0.85
    return {
        'x': nrm((BATCH, SEQ, D), 1.0),
        'mem': nrm((BATCH, N_MEM, D), 1.0),
        'mem_norm': gain((D,)),
        'ln1': gain((DEPTH, D)),
        'ln2': gain((DEPTH, D)),
        'w_out': nrm((DEPTH, D, D), D ** -0.5),
        'w_mem_kv': nrm((DEPTH, D, 2 * D_MEMQ), D ** -0.5),
        'ffn_in': nrm((DEPTH, D, 2 * D_FF), D ** -0.5),
        'ffn_conv': nrm((DEPTH, CONV_W, D_FF), CONV_W ** -0.5),
        'ffn_conv_b': nrm((DEPTH, D_FF), 0.02),
        'ffn_out': nrm((DEPTH, D_FF, D), D_FF ** -0.5),
        'a_w_in': nrm((NA, D, 3 * D_MIX + D_MEMQ), D ** -0.5),
        'a_mu_rkv': unif((NA, 3, D_MIX)),
        'a_mu_x': unif((NA, 3, D)),
        'a_w0': w0_base[None, :] + nrm((NA, D_MIX), 0.1),
        'a_w1': nrm((NA, D, LORA_W), D ** -0.5),
        'a_w2': nrm((NA, LORA_W, D_MIX), 0.1 * LORA_W ** -0.5),
        'a_a0': nrm((NA, D_MIX), 0.1),
        'a_a1': nrm((NA, D, LORA_A), D ** -0.5),
        'a_a2': nrm((NA, LORA_A, D_MIX), 0.1 * LORA_A ** -0.5),
        'a_g1': nrm((NA, D, LORA_G), D ** -0.5),
        'a_g2': nrm((NA, LORA_G, D_MIX), LORA_G ** -0.5),
        'a_k_k': 0.85 + nrm((NA, D_MIX), 0.02),
        'a_k_a': gain((NA, D_MIX)),
        'a_r_k': nrm((NA, N_MIX_HEADS, HEAD_DIM), 0.1),
        'a_lnx_w': gain((NA, D_MIX)),
        'a_lnx_b': nrm((NA, D_MIX), 0.02),
        'a_mu_v': unif((NV, D)),
        'a_v0': 1.0 + nrm((NV, D_MIX), 0.1),
        'a_v1': nrm((NV, D, LORA_V), D ** -0.5),
        'a_v2': nrm((NV, LORA_V, D_MIX), 0.1 * LORA_V ** -0.5),
        'ln_kv': gain((D,)),
        'w_kv': nrm((D, 2 * D_MIX), D ** -0.5),
        'b_w_in': nrm((NB, D, D_MIX + D_MEMQ), D ** -0.5),
        'b_rel': nrm((NB, N_MIX_HEADS, REL_TABLE), 0.2),
        'ln_f': gain((D,)),
    }


def reference(x, mem, mem_norm, ln1, ln2, w_out, w_mem_kv, ffn_in, ffn_conv, ffn_conv_b,
              ffn_out, a_w_in, a_mu_rkv, a_mu_x, a_w0, a_w1, a_w2, a_a0, a_a1, a_a2,
              a_g1, a_g2, a_k_k, a_k_a, a_r_k, a_lnx_w, a_lnx_b, a_mu_v, a_v0, a_v1, a_v2,
              ln_kv, w_kv, b_w_in, b_rel, ln_f):
    bsz, seq, _ = x.shape
    mem_n = rms_norm(mem, mem_norm)
    v_first = None
    k_pad = v_pad = None
    for layer in range(DEPTH):
        if layer < N_A_LAYERS:
            i = layer
            h = rms_norm(x, ln1[layer])
            p = h @ a_w_in[i]
            v_res = None if i == 0 else (a_mu_v[i - 1], a_v0[i - 1], a_v1[i - 1], a_v2[i - 1])
            mix, v_first = rwkv7_time_mix(
                h, p[..., :3 * D_MIX], a_mu_rkv[i], a_mu_x[i], a_w0[i], a_w1[i], a_w2[i],
                a_a0[i], a_a1[i], a_a2[i], a_g1[i], a_g2[i], a_k_k[i], a_k_a[i], a_r_k[i],
                a_lnx_w[i], a_lnx_b[i], v_first, v_res)
            q_mem = p[..., 3 * D_MIX:]
        else:
            j = layer - N_A_LAYERS
            if j == 0:
                k_s, v_s = jnp.split(rms_norm(x, ln_kv) @ w_kv, 2, axis=-1)

                def to_band(t):
                    t = t.reshape(bsz, seq, N_MIX_HEADS, HEAD_DIM).transpose(0, 2, 1, 3)
                    return jnp.pad(t, ((0, 0), (0, 0), (PAD_LEN, 0), (0, 0)))

                k_pad, v_pad = to_band(k_s), to_band(v_s)
            h = rms_norm(x, ln1[layer])
            p = h @ b_w_in[j]
            q = p[..., :D_MIX].reshape(bsz, seq, N_MIX_HEADS, HEAD_DIM).transpose(0, 2, 1, 3)
            mix = chunk_attention(q, k_pad, v_pad, rel_bias_band(b_rel[j]))
            q_mem = p[..., D_MIX:]
        mo = memory_attention(q_mem, mem_n, w_mem_kv[layer])
        x = x + jnp.concatenate([mix, mo], axis=-1) @ w_out[layer]
        x = x + conv_glu(rms_norm(x, ln2[layer]), ffn_in[layer], ffn_conv[layer],
                         ffn_conv_b[layer], ffn_out[layer])
    return rms_norm(x, ln_f)
```

```python
import functools
import math

import jax
import jax.numpy as jnp
from jax import lax
from jax.experimental import pallas as pl
from jax.experimental.pallas import tpu as pltpu

F32 = jnp.float32
BF16 = jnp.bfloat16

HEAD_DIM = 64
LANES = 128
HEADS_PER_GROUP = LANES // HEAD_DIM
SUBLANES = 8
N_MEM_HEADS = 4
CHUNK = 64
LEFT_CHUNKS = 8
REL_CLIP = 256
CONV_W = 3
LN_X_EPS = 64e-5
RMS_EPS = 1e-6
MASK_VALUE = -1e30

SCAN_CHUNK = 128
SCAN_CHUNKS_PER_STEP = 1
ROW_TILE = 512
ATTN_TILE = LEFT_CHUNKS * CHUNK
FFN_COLS = 256
VMEM_LIMIT = 56 * 1024 * 1024

NT = (((1,), (1,)), ((), ()))
TN = (((0,), (0,)), ((), ()))
NN = (((1,), (0,)), ((), ()))


def _bdot(a, b, dims=NN):
    return lax.dot_general(a.astype(BF16), b.astype(BF16), dims,
                           preferred_element_type=F32)


def _split(x):
    hi = x.astype(BF16)
    lo = (x - hi.astype(F32)).astype(BF16)
    return hi, lo


def _ones_dot_right(x, ones_bf16):
    hi, lo = _split(x)
    return (jnp.dot(hi, ones_bf16, preferred_element_type=F32)
            + jnp.dot(lo, ones_bf16, preferred_element_type=F32))


def _ones_dot_left(ones_bf16, x):
    hi, lo = _split(x)
    return (jnp.dot(ones_bf16, hi, preferred_element_type=F32)
            + jnp.dot(ones_bf16, lo, preferred_element_type=F32))


def _rms_scale(x):
    return x * lax.rsqrt(jnp.mean(x * x, axis=-1, keepdims=True) + RMS_EPS)


def _shift_rows(t, prev_rows, n):
    rolled = pltpu.roll(t, n, 0)
    row = lax.broadcasted_iota(jnp.int32, t.shape, 0)
    out = rolled
    for j in range(n):
        out = jnp.where(row == j, prev_rows[j:j + 1, :], out)
    return out


def _const_spec(shape):
    nd = len(shape)
    return pl.BlockSpec(shape, lambda *_: (0,) * nd, pipeline_mode=pl.Buffered(1))


def _params(*sem):
    return pltpu.CompilerParams(dimension_semantics=sem, vmem_limit_bytes=VMEM_LIMIT)


def _mem_kv_kernel(mem_ref, g_ref, w_ref, out_ref):
    mem_n = _rms_scale(mem_ref[...]) * g_ref[...]
    out_ref[...] = _bdot(mem_n, w_ref[...]).astype(out_ref.dtype)


def _mem_kv(mem, mem_norm, w_mem_kv):
    depth, d, n = w_mem_kv.shape
    n_mem = mem.shape[0]
    return pl.pallas_call(
        _mem_kv_kernel,
        grid=(depth,),
        in_specs=[pl.BlockSpec((n_mem, d), lambda l: (0, 0)),
                  pl.BlockSpec((1, d), lambda l: (0, 0)),
                  pl.BlockSpec((None, d, n), lambda l: (l, 0, 0))],
        out_specs=pl.BlockSpec((None, n_mem, n), lambda l: (l, 0, 0)),
        out_shape=jax.ShapeDtypeStruct((depth, n_mem, n), BF16),
        compiler_params=_params("arbitrary"),
    )(mem, mem_norm.reshape(1, d), w_mem_kv.astype(BF16))


def _softplus(u):
    return jnp.maximum(u, 0.0) + jnp.log1p(jnp.exp(-jnp.abs(u)))


def _rwkv_pre_kernel(has_vres, d_mix, *refs):
    if has_vres:
        (x_ref, ln_ref, win_ref, wlh_ref, wld_ref, w2_ref, a2_ref, g2_ref, mu_ref,
         w0_ref, a0_ref, v2_ref, v0_ref, vf_ref,
         r_ref, lw_ref, k_ref, v_ref, a_ref, g_ref, qm_ref, hprev, pprev) = refs
    else:
        (x_ref, ln_ref, win_ref, wlh_ref, wld_ref, w2_ref, a2_ref, g2_ref, mu_ref,
         w0_ref, a0_ref,
         r_ref, lw_ref, k_ref, v_ref, a_ref, g_ref, qm_ref, hprev, pprev) = refs

    @pl.when(pl.program_id(0) == 0)
    def _():
        hprev[...] = jnp.zeros_like(hprev)
        pprev[...] = jnp.zeros_like(pprev)

    tm = x_ref.shape[0]
    h = _rms_scale(x_ref[...]) * ln_ref[...]
    dh = _shift_rows(h, hprev[...], 1) - h
    hprev[...] = h[tm - 1:tm, :]
    hb = h.astype(BF16)
    p = jnp.dot(hb, win_ref[...], preferred_element_type=F32)
    l1 = (jnp.dot(hb, wlh_ref[...], preferred_element_type=F32)
          + jnp.dot(dh.astype(BF16), wld_ref[...], preferred_element_type=F32))

    prkv = p[:, :3 * d_mix]
    ps = _shift_rows(prkv, pprev[...], 1)
    pprev[...] = prkv[tm - 1:tm, :]
    mu = mu_ref[...]

    def lerp(j):
        cur = prkv[:, j * d_mix:(j + 1) * d_mix]
        return cur + (ps[:, j * d_mix:(j + 1) * d_mix] - cur) * mu[j:j + 1, :]

    r_ref[...] = lerp(0)
    k_ref[...] = lerp(1)
    v = lerp(2)
    qm_ref[...] = p[:, 3 * d_mix:].astype(qm_ref.dtype)

    l_wa = l1[:, :LANES]
    z = w0_ref[...] + _bdot(jnp.tanh(l_wa), w2_ref[...])
    w_log = -_softplus(-z) - 0.5
    lw_ref[...] = -jnp.exp(w_log)
    a_ref[...] = jax.nn.sigmoid(a0_ref[...] + _bdot(l_wa, a2_ref[...]))
    g_ref[...] = _bdot(jax.nn.sigmoid(l1[:, LANES:2 * LANES]), g2_ref[...])
    if has_vres:
        gate = jax.nn.sigmoid(v0_ref[...] + _bdot(l1[:, 2 * LANES:3 * LANES], v2_ref[...]))
        v = v + (vf_ref[...] - v) * gate
    v_ref[...] = v


def _pad_rows(w, rows, offset):
    out = jnp.zeros((rows, w.shape[1]), w.dtype)
    return out.at[offset:offset + w.shape[0]].set(w)


def _rwkv_pre(x, ln1, w_in, mu_rkv, mu_x, w0, w1, w2, a0, a1, a2, g1, g2, vres, v_first):
    seq, d = x.shape
    d_mix = w0.shape[0]
    d_memq = w_in.shape[1] - 3 * d_mix
    lora_w, lora_a, lora_g = w1.shape[1], a1.shape[1], g1.shape[1]
    assert lora_w + lora_a == LANES and lora_g == LANES
    has_vres = vres is not None
    firsts = [w1, a1, g1]
    mus = [mu_x[0], mu_x[1], mu_x[2]]
    if has_vres:
        mu_v, v0, v1, v2 = vres
        assert v1.shape[1] <= LANES
        firsts.append(jnp.pad(v1, ((0, 0), (0, LANES - v1.shape[1]))))
        mus.append(mu_v)
    wl_h = jnp.concatenate(firsts, axis=1)
    wl_d = jnp.concatenate([m[:, None] * w for m, w in zip(mus, firsts)], axis=1)
    lp = wl_h.shape[1]
    w2p = _pad_rows(w2, LANES, 0)
    a2p = _pad_rows(a2, LANES, lora_w)

    tm = min(ROW_TILE, seq)
    assert seq % tm == 0
    row = lambda n: pl.BlockSpec((tm, n), lambda i: (i, 0))
    ins = [x, ln1.reshape(1, d), w_in.astype(BF16), wl_h.astype(BF16), wl_d.astype(BF16),
           w2p.astype(BF16), a2p.astype(BF16), g2.astype(BF16), mu_rkv,
           w0.reshape(1, d_mix), a0.reshape(1, d_mix)]
    specs = [row(d), _const_spec((1, d)), _const_spec(w_in.shape), _const_spec((d, lp)),
             _const_spec((d, lp)), _const_spec((LANES, d_mix)), _const_spec((LANES, d_mix)),
             _const_spec((LANES, d_mix)), _const_spec(mu_rkv.shape),
             _const_spec((1, d_mix)), _const_spec((1, d_mix))]
    if has_vres:
        ins += [_pad_rows(v2, LANES, 0).astype(BF16), v0.reshape(1, d_mix), v_first]
        specs += [_const_spec((LANES, d_mix)), _const_spec((1, d_mix)), row(d_mix)]
    mix_out = jax.ShapeDtypeStruct((seq, d_mix), F32)
    return pl.pallas_call(
        functools.partial(_rwkv_pre_kernel, has_vres, d_mix),
        grid=(seq // tm,),
        in_specs=specs,
        out_specs=[row(d_mix)] * 6 + [row(d_memq)],
        out_shape=[mix_out] * 6 + [jax.ShapeDtypeStruct((seq, d_memq), BF16)],
        scratch_shapes=[pltpu.VMEM((1, d), F32), pltpu.VMEM((1, 3 * d_mix), F32)],
        compiler_params=_params("arbitrary"),
    )(*ins)


def _scan_chunk(r, lw, k, v, a, kk_w, ka_w, rk_w, lnw, lnb, state, consts):
    tri_incl_b, strict, incl, eye, head_ones, same_head, head0 = consts
    n = r.shape[0]

    kk = k * kk_w
    ss = _ones_dot_right(kk * kk, head_ones)
    kk = kk / jnp.maximum(jnp.sqrt(ss), 1e-12)
    kmod = k * (1.0 + (a - 1.0) * ka_w)
    bvec = kk * a

    c = _ones_dot_left(tri_incl_b, lw)
    ref = c[n // 2 - 1:n // 2, :]
    e_last = jnp.exp(c[n - 1:n, :] - ref)
    dec = jnp.exp(c - ref)
    inv = jnp.exp(ref - c)
    rt = r * dec
    at = -kk * jnp.exp(c - lw - ref)
    bt = (bvec * inv).astype(BF16)
    kt = (kmod * inv).astype(BF16)
    vb = v.astype(BF16)

    per_head = []
    for hd in range(HEADS_PER_GROUP):
        mh = head0 if hd == 0 else jnp.logical_not(head0)
        at_h = jnp.where(mh, at, 0.0).astype(BF16)
        rt_h = jnp.where(mh, rt, 0.0).astype(BF16)
        ab = jnp.where(strict, _bdot(at_h, bt, NT), 0.0)
        ak = jnp.where(strict, _bdot(at_h, kt, NT), 0.0)
        rb = jnp.where(incl, _bdot(rt_h, bt, NT), 0.0)
        rkm = jnp.where(incl, _bdot(rt_h, kt, NT), 0.0)
        t_inv = eye + ab
        npow = ab
        for _ in range(int(math.log2(n)) - 1):
            npow = _bdot(npow, npow)
            t_inv = t_inv + _bdot(t_inv, npow)
        per_head.append((t_inv, rb, rkm, _bdot(ak, vb)))

    sel = lambda x0, x1: jnp.where(head0, x0, x1)
    akv = sel(per_head[0][3], per_head[1][3])
    ahat = sel(_bdot(per_head[0][0], at), _bdot(per_head[1][0], at))
    uin = sel(_bdot(per_head[0][0], akv), _bdot(per_head[1][0], akv))
    rhat = rt + sel(_bdot(per_head[0][1], ahat), _bdot(per_head[1][1], ahat))
    yin = sel(_bdot(per_head[0][1], uin) + _bdot(per_head[0][2], vb),
              _bdot(per_head[1][1], uin) + _bdot(per_head[1][2], vb))

    g1 = jnp.where(same_head, _bdot(ahat, bt, TN), 0.0)
    m_mat = (eye + g1) * e_last
    c_mat = jnp.where(same_head, _bdot(uin, bt, TN) + _bdot(vb, kt, TN), 0.0) * e_last

    sp = state * jnp.exp(ref)
    y = _bdot(rhat, sp, NT) + yin
    new_state = _bdot(sp, m_mat) + c_mat

    inv_hd = 1.0 / HEAD_DIM
    mean = _ones_dot_right(y, head_ones) * inv_hd
    dy = y - mean
    var = _ones_dot_right(dy * dy, head_ones) * inv_hd
    yn = dy * lax.rsqrt(var + LN_X_EPS) * lnw + lnb
    bonus = _ones_dot_right(r * kmod * rk_w, head_ones)
    return yn + bonus * v, new_state


def _rwkv_scan_kernel(r_ref, lw_ref, k_ref, v_ref, a_ref, kk_ref, ka_ref, rk_ref,
                      lnw_ref, lnb_ref, y_ref, state_ref):
    @pl.when(pl.program_id(1) == 0)
    def _():
        state_ref[...] = jnp.zeros_like(state_ref)

    n = SCAN_CHUNK
    row = lax.broadcasted_iota(jnp.int32, (n, n), 0)
    col = lax.broadcasted_iota(jnp.int32, (n, n), 1)
    incl = row >= col
    strict = row > col
    same_head = (row // HEAD_DIM) == (col // HEAD_DIM)
    consts = (incl.astype(BF16), strict, incl, (row == col).astype(F32),
              same_head.astype(BF16), same_head, col < HEAD_DIM)

    state = state_ref[...]
    for j in range(r_ref.shape[0] // n):
        rows = pl.ds(j * n, n)
        y, state = _scan_chunk(r_ref[rows, :], lw_ref[rows, :], k_ref[rows, :], v_ref[rows, :],
                               a_ref[rows, :], kk_ref[...], ka_ref[...], rk_ref[...],
                               lnw_ref[...], lnb_ref[...], state, consts)
        y_ref[rows, :] = y
    state_ref[...] = state


def _rwkv_scan(r, lw, k, v, a, k_k, k_a, r_k, lnx_w, lnx_b):
    seq, d_mix = r.shape
    assert LANES == SCAN_CHUNK and d_mix % LANES == 0
    rows = min(SCAN_CHUNK * SCAN_CHUNKS_PER_STEP, seq)
    assert seq % rows == 0 and rows % SCAN_CHUNK == 0
    act = pl.BlockSpec((rows, LANES), lambda p, c: (c, p))
    par = pl.BlockSpec((1, LANES), lambda p, c: (0, p))
    flat = lambda t: t.reshape(1, d_mix)
    return pl.pallas_call(
        _rwkv_scan_kernel,
        grid=(d_mix // LANES, seq // rows),
        in_specs=[act] * 5 + [par] * 5,
        out_specs=act,
        out_shape=jax.ShapeDtypeStruct((seq, d_mix), F32),
        scratch_shapes=[pltpu.VMEM((LANES, LANES), F32)],
        compiler_params=_params("parallel", "arbitrary"),
    )(r, lw, k, v, a, flat(k_k), flat(k_a), flat(r_k), flat(lnx_w), flat(lnx_b))


def _norm_proj_kernel(n_proj, *refs):
    x_ref = refs[0]
    gains = refs[1:1 + n_proj]
    weights = refs[1 + n_proj:1 + 2 * n_proj]
    outs = refs[1 + 2 * n_proj:]
    xn = _rms_scale(x_ref[...])
    o = 0
    for g_ref, w_ref in zip(gains, weights):
        y = _bdot(xn * g_ref[...], w_ref[...])
        start = 0
        while start < y.shape[1]:
            width = outs[o].shape[1]
            outs[o][...] = y[:, start:start + width].astype(outs[o].dtype)
            start += width
            o += 1


def _norm_proj(x, projections):
    seq, d = x.shape
    tm = min(ROW_TILE, seq)
    row = lambda n: pl.BlockSpec((tm, n), lambda i: (i, 0))
    gains = [g.reshape(1, d) for g, _, _ in projections]
    weights = [w.astype(BF16) for _, w, _ in projections]
    widths = [n for _, _, ws in projections for n in ws]
    return pl.pallas_call(
        functools.partial(_norm_proj_kernel, len(projections)),
        grid=(seq // tm,),
        in_specs=([row(d)] + [_const_spec((1, d))] * len(gains)
                  + [_const_spec(w.shape) for w in weights]),
        out_specs=[row(n) for n in widths],
        out_shape=[jax.ShapeDtypeStruct((seq, n), BF16) for n in widths],
        compiler_params=_params("arbitrary"),
    )(x, *gains, *weights)


def _band_bias(table):
    qi = jnp.arange(ATTN_TILE)[:, None]
    kj = jnp.arange(2 * ATTN_TILE)[None, :]
    band_start = (qi // CHUNK) * CHUNK
    visible = (kj >= band_start) & (kj < band_start + (LEFT_CHUNKS + 1) * CHUNK)
    idx = jnp.clip(qi + ATTN_TILE - kj, -(CHUNK - 1), REL_CLIP) + (CHUNK - 1)
    return jnp.where(visible[None], table[:, idx].astype(F32), MASK_VALUE)


def _chunk_attn_kernel(q_ref, kp_ref, kc_ref, vp_ref, vc_ref, bias_ref, o_ref):
    tq = q_ref.shape[0]
    has_prev = pl.program_id(1) > 0
    q = q_ref[...]
    lane = lax.broadcasted_iota(jnp.int32, q.shape, 1)
    scale = HEAD_DIM ** -0.5
    out = None
    for hd in range(HEADS_PER_GROUP):
        mh = (lane // HEAD_DIM) == hd
        qh = jnp.where(mh, q, jnp.zeros_like(q))
        s_prev = _bdot(qh, kp_ref[...], NT) * scale + bias_ref[hd, :, :tq]
        s_prev = jnp.where(has_prev, s_prev, MASK_VALUE)
        s_cur = _bdot(qh, kc_ref[...], NT) * scale + bias_ref[hd, :, tq:]
        m = jnp.maximum(jnp.max(s_prev, axis=-1, keepdims=True),
                        jnp.max(s_cur, axis=-1, keepdims=True))
        e_prev = jnp.exp(s_prev - m)
        e_cur = jnp.exp(s_cur - m)
        denom = (jnp.sum(e_prev, axis=-1, keepdims=True)
                 + jnp.sum(e_cur, axis=-1, keepdims=True))
        o_h = (_bdot(e_prev, vp_ref[...]) + _bdot(e_cur, vc_ref[...])) / denom
        out = o_h if out is None else jnp.where(mh, o_h, out)
    o_ref[...] = out


def _chunk_attn(q, k, v, rel_table):
    seq, d_mix = q.shape
    tq = ATTN_TILE
    assert seq % tq == 0
    cur = pl.BlockSpec((tq, LANES), lambda p, i: (i, p))
    prev = pl.BlockSpec((tq, LANES), lambda p, i: (jnp.maximum(i - 1, 0), p))
    bias = _band_bias(rel_table)
    return pl.pallas_call(
        _chunk_attn_kernel,
        grid=(d_mix // LANES, seq // tq),
        in_specs=[cur, prev, cur, prev, cur,
                  pl.BlockSpec((HEADS_PER_GROUP, tq, 2 * tq), lambda p, i: (p, 0, 0))],
        out_specs=cur,
        out_shape=jax.ShapeDtypeStruct((seq, d_mix), F32),
        compiler_params=_params("parallel", "arbitrary"),
    )(q, k, k, v, v, bias)


def _layer_post_kernel(has_gate, has_final, d_mix, d_ff, *refs):
    refs = list(refs)
    x_ref, mix_ref = refs[:2]
    refs = refs[2:]
    gate_ref = refs.pop(0) if has_gate else None
    (qm_ref, kvm_ref, wout_ref, ln2_ref, fin_ref, cw_ref, cb_ref, fout_ref) = refs[:8]
    refs = refs[8:]
    lnf_ref = refs.pop(0) if has_final else None
    out_ref, carry = refs

    @pl.when(pl.program_id(0) == 0)
    def _():
        carry[...] = jnp.zeros_like(carry)

    tm = x_ref.shape[0]
    d_memq = qm_ref.shape[1]
    mix = mix_ref[...]
    if has_gate:
        mix = mix * gate_ref[...]

    qm = qm_ref[...]
    km = kvm_ref[:, :d_memq]
    vm = kvm_ref[:, d_memq:]
    lane = lax.broadcasted_iota(jnp.int32, qm.shape, 1)
    mo = None
    for hd in range(d_memq // HEAD_DIM):
        mh = (lane // HEAD_DIM) == hd
        s = _bdot(jnp.where(mh, qm, jnp.zeros_like(qm)), km, NT) * HEAD_DIM ** -0.5
        e = jnp.exp(s - jnp.max(s, axis=-1, keepdims=True))
        o_h = _bdot(e, vm) / jnp.sum(e, axis=-1, keepdims=True)
        mo = o_h if mo is None else jnp.where(mh, o_h, mo)

    x1 = (x_ref[...] + _bdot(mix, wout_ref[:d_mix, :]) + _bdot(mo, wout_ref[d_mix:, :]))

    h2 = (_rms_scale(x1) * ln2_ref[...]).astype(BF16)
    ffn = None
    for c0 in range(0, d_ff, FFN_COLS):
        cols = slice(c0, c0 + FFN_COLS)
        gate = jnp.dot(h2, fin_ref[:, cols], preferred_element_type=F32)
        val = jnp.dot(h2, fin_ref[:, d_ff + c0:d_ff + c0 + FFN_COLS], preferred_element_type=F32)
        prev = carry[SUBLANES - (CONV_W - 1):, cols]
        carry[:, cols] = gate[tm - SUBLANES:, :]
        conv = cb_ref[:, cols] + cw_ref[CONV_W - 1:CONV_W, cols] * gate
        for j in range(1, CONV_W):
            conv = conv + (cw_ref[CONV_W - 1 - j:CONV_W - j, cols]
                           * _shift_rows(gate, prev[CONV_W - 1 - j:, :], j))
        act = 0.5 * conv * (1.0 + lax.erf(conv * (2.0 ** -0.5))) * val
        slab = _bdot(act, fout_ref[cols, :])
        ffn = slab if ffn is None else ffn + slab
    acc = x1 + ffn
    if has_final:
        acc = _rms_scale(acc) * lnf_ref[...]
    out_ref[...] = acc


def _layer_post(x, mix, gate, qm, kv_mem, w_out, ln2, ffn_in, conv_w, conv_b, ffn_out, ln_f):
    seq, d = x.shape
    d_mix = mix.shape[1]
    d_memq = qm.shape[1]
    d_ff = ffn_out.shape[0]
    assert d_ff % FFN_COLS == 0 and conv_w.shape[0] == CONV_W
    tm = min(ROW_TILE, seq)
    row = lambda n: pl.BlockSpec((tm, n), lambda i: (i, 0))
    ins, specs = [x, mix], [row(d), row(d_mix)]
    if gate is not None:
        ins.append(gate)
        specs.append(row(d_mix))
    ins += [qm, kv_mem, w_out.astype(BF16), ln2.reshape(1, d), ffn_in.astype(BF16),
            conv_w, conv_b.reshape(1, d_ff), ffn_out.astype(BF16)]
    specs += [row(d_memq), _const_spec(kv_mem.shape), _const_spec(w_out.shape),
              _const_spec((1, d)), _const_spec(ffn_in.shape), _const_spec(conv_w.shape),
              _const_spec((1, d_ff)), _const_spec(ffn_out.shape)]
    if ln_f is not None:
        ins.append(ln_f.reshape(1, d))
        specs.append(_const_spec((1, d)))
    return pl.pallas_call(
        functools.partial(_layer_post_kernel, gate is not None, ln_f is not None, d_mix, d_ff),
        grid=(seq // tm,),
        in_specs=specs,
        out_specs=row(d),
        out_shape=jax.ShapeDtypeStruct((seq, d), F32),
        scratch_shapes=[pltpu.VMEM((SUBLANES, d_ff), F32)],
        compiler_params=_params("arbitrary"),
    )(*ins)


def kernel(x, mem, mem_norm, ln1, ln2, w_out, w_mem_kv, ffn_in, ffn_conv, ffn_conv_b, ffn_out, a_w_in, a_mu_rkv, a_mu_x, a_w0, a_w1, a_w2, a_a0, a_a1, a_a2, a_g1, a_g2, a_k_k, a_k_a, a_r_k, a_lnx_w, a_lnx_b, a_mu_v, a_v0, a_v1, a_v2, ln_kv, w_kv, b_w_in, b_rel, ln_f):
    bsz, seq, d = x.shape
    assert bsz == 1 and mem.shape[0] == 1
    depth = ln1.shape[0]
    n_a = a_w_in.shape[0]
    d_mix = a_w0.shape[1]
    xs = x.reshape(seq, d)
    kv_mem = _mem_kv(mem.reshape(mem.shape[1], d), mem_norm, w_mem_kv)

    v_first = None
    k_s = v_s = None
    for layer in range(depth):
        last = ln_f if layer == depth - 1 else None
        if layer < n_a:
            i = layer
            vres = None if i == 0 else (a_mu_v[i - 1], a_v0[i - 1], a_v1[i - 1], a_v2[i - 1])
            r, lw, k, v, a, g, qm = _rwkv_pre(
                xs, ln1[layer], a_w_in[i], a_mu_rkv[i], a_mu_x[i], a_w0[i], a_w1[i], a_w2[i],
                a_a0[i], a_a1[i], a_a2[i], a_g1[i], a_g2[i], vres, v_first)
            if i == 0:
                v_first = v
            mix = _rwkv_scan(r, lw, k, v, a, a_k_k[i], a_k_a[i], a_r_k[i],
                             a_lnx_w[i], a_lnx_b[i])
            gate = g
        else:
            j = layer - n_a
            d_memq = b_w_in.shape[2] - d_mix
            projections = [(ln1[layer], b_w_in[j], [d_mix, d_memq])]
            if j == 0:
                projections.append((ln_kv, w_kv, [d_mix, d_mix]))
            outs = _norm_proj(xs, projections)
            q, qm = outs[:2]
            if j == 0:
                k_s, v_s = outs[2:]
            mix = _chunk_attn(q, k_s, v_s, b_rel[j])
            gate = None
        xs = _layer_post(xs, mix, gate, qm, kv_mem[layer], w_out[layer], ln2[layer],
                         ffn_in[layer], ffn_conv[layer], ffn_conv_b[layer], ffn_out[layer], last)
    return xs.reshape(bsz, seq, d)
```

```python
import functools
import math

import jax
import jax.numpy as jnp
from jax import lax
from jax.experimental import pallas as pl
from jax.experimental.pallas import tpu as pltpu

F32 = jnp.float32
BF16 = jnp.bfloat16

HEAD_DIM = 64
LANES = 128
HEADS_PER_GROUP = LANES // HEAD_DIM
SUBLANES = 8
N_MEM_HEADS = 4
CHUNK = 64
LEFT_CHUNKS = 8
BAND = (LEFT_CHUNKS + 1) * CHUNK
REL_CLIP = 256
CONV_W = 3
LN_X_EPS = 64e-5
RMS_EPS = 1e-6
MASK_VALUE = -1e30

SCAN_CHUNK = 128
SCAN_CHUNKS_PER_STEP = 4
ROW_TILE = 512
ATTN_TILE = LEFT_CHUNKS * CHUNK
FFN_COLS = 256
VMEM_LIMIT = 56 * 1024 * 1024

NT = (((1,), (1,)), ((), ()))
TN = (((0,), (0,)), ((), ()))
NN = (((1,), (0,)), ((), ()))


def _bdot(a, b, dims=NN):
    return lax.dot_general(a.astype(BF16), b.astype(BF16), dims,
                           preferred_element_type=F32)


def _split(x):
    hi = x.astype(BF16)
    lo = (x - hi.astype(F32)).astype(BF16)
    return hi, lo


def _ones_dot_right(x, ones_bf16):
    hi, lo = _split(x)
    return (jnp.dot(hi, ones_bf16, preferred_element_type=F32)
            + jnp.dot(lo, ones_bf16, preferred_element_type=F32))


def _ones_dot_left(ones_bf16, x):
    hi, lo = _split(x)
    return (jnp.dot(ones_bf16, hi, preferred_element_type=F32)
            + jnp.dot(ones_bf16, lo, preferred_element_type=F32))


def _rms_scale(x):
    return x * lax.rsqrt(jnp.mean(x * x, axis=-1, keepdims=True) + RMS_EPS)


def _shift_rows(t, prev_rows, n):
    rolled = pltpu.roll(t, n, 0)
    row = lax.broadcasted_iota(jnp.int32, t.shape, 0)
    out = rolled
    for j in range(n):
        out = jnp.where(row == j, prev_rows[j:j + 1, :], out)
    return out


def _const_spec(shape):
    nd = len(shape)
    return pl.BlockSpec(shape, lambda *_: (0,) * nd, pipeline_mode=pl.Buffered(1))


def _params(*sem):
    return pltpu.CompilerParams(dimension_semantics=sem, vmem_limit_bytes=VMEM_LIMIT)


def _mem_kv_kernel(mem_ref, g_ref, w_ref, out_ref):
    mem_n = _rms_scale(mem_ref[...]) * g_ref[...]
    out_ref[...] = _bdot(mem_n, w_ref[...]).astype(out_ref.dtype)


def _mem_kv(mem, mem_norm, w_mem_kv):
    depth, d, n = w_mem_kv.shape
    n_mem = mem.shape[0]
    return pl.pallas_call(
        _mem_kv_kernel,
        grid=(depth,),
        in_specs=[pl.BlockSpec((n_mem, d), lambda l: (0, 0)),
                  pl.BlockSpec((1, d), lambda l: (0, 0)),
                  pl.BlockSpec((None, d, n), lambda l: (l, 0, 0))],
        out_specs=pl.BlockSpec((None, n_mem, n), lambda l: (l, 0, 0)),
        out_shape=jax.ShapeDtypeStruct((depth, n_mem, n), BF16),
        compiler_params=_params("arbitrary"),
    )(mem, mem_norm.reshape(1, d), w_mem_kv.astype(BF16))


def _softplus(u):
    return jnp.maximum(u, 0.0) + jnp.log1p(jnp.exp(-jnp.abs(u)))


def _rwkv_pre_kernel(has_vres, d_mix, *refs):
    if has_vres:
        (x_ref, ln_ref, win_ref, wlh_ref, wld_ref, w2_ref, a2_ref, g2_ref, mu_ref,
         w0_ref, a0_ref, v2_ref, v0_ref, vf_ref,
         r_ref, lw_ref, k_ref, v_ref, a_ref, g_ref, qm_ref, hprev, pprev) = refs
    else:
        (x_ref, ln_ref, win_ref, wlh_ref, wld_ref, w2_ref, a2_ref, g2_ref, mu_ref,
         w0_ref, a0_ref,
         r_ref, lw_ref, k_ref, v_ref, a_ref, g_ref, qm_ref, hprev, pprev) = refs

    @pl.when(pl.program_id(0) == 0)
    def _():
        hprev[...] = jnp.zeros_like(hprev)
        pprev[...] = jnp.zeros_like(pprev)

    tm = x_ref.shape[0]
    h = _rms_scale(x_ref[...]) * ln_ref[...]
    dh = _shift_rows(h, hprev[...], 1) - h
    hprev[...] = h[tm - 1:tm, :]
    hb = h.astype(BF16)
    p = jnp.dot(hb, win_ref[...], preferred_element_type=F32)
    l1 = (jnp.dot(hb, wlh_ref[...], preferred_element_type=F32)
          + jnp.dot(dh.astype(BF16), wld_ref[...], preferred_element_type=F32))

    prkv = p[:, :3 * d_mix]
    ps = _shift_rows(prkv, pprev[...], 1)
    pprev[...] = prkv[tm - 1:tm, :]
    mu = mu_ref[...]

    def lerp(j):
        cur = prkv[:, j * d_mix:(j + 1) * d_mix]
        return cur + (ps[:, j * d_mix:(j + 1) * d_mix] - cur) * mu[j:j + 1, :]

    r_ref[...] = lerp(0)
    k_ref[...] = lerp(1)
    v = lerp(2)
    qm_ref[...] = p[:, 3 * d_mix:].astype(qm_ref.dtype)

    l_wa = l1[:, :LANES]
    z = w0_ref[...] + _bdot(jnp.tanh(l_wa), w2_ref[...])
    w_log = -_softplus(-z) - 0.5
    lw_ref[...] = -jnp.exp(w_log)
    a_ref[...] = jax.nn.sigmoid(a0_ref[...] + _bdot(l_wa, a2_ref[...]))
    g_ref[...] = _bdot(jax.nn.sigmoid(l1[:, LANES:2 * LANES]), g2_ref[...])
    if has_vres:
        gate = jax.nn.sigmoid(v0_ref[...] + _bdot(l1[:, 2 * LANES:3 * LANES], v2_ref[...]))
        v = v + (vf_ref[...] - v) * gate
    v_ref[...] = v


def _pad_rows(w, rows, offset):
    out = jnp.zeros((rows, w.shape[1]), w.dtype)
    return out.at[offset:offset + w.shape[0]].set(w)


def _rwkv_pre(x, ln1, w_in, mu_rkv, mu_x, w0, w1, w2, a0, a1, a2, g1, g2, vres, v_first):
    seq, d = x.shape
    d_mix = w0.shape[0]
    d_memq = w_in.shape[1] - 3 * d_mix
    lora_w, lora_a, lora_g = w1.shape[1], a1.shape[1], g1.shape[1]
    assert lora_w + lora_a == LANES and lora_g == LANES
    has_vres = vres is not None
    firsts = [w1, a1, g1]
    mus = [mu_x[0], mu_x[1], mu_x[2]]
    if has_vres:
        mu_v, v0, v1, v2 = vres
        assert v1.shape[1] <= LANES
        firsts.append(jnp.pad(v1, ((0, 0), (0, LANES - v1.shape[1]))))
        mus.append(mu_v)
    wl_h = jnp.concatenate(firsts, axis=1)
    wl_d = jnp.concatenate([m[:, None] * w for m, w in zip(mus, firsts)], axis=1)
    lp = wl_h.shape[1]
    w2p = _pad_rows(w2, LANES, 0)
    a2p = _pad_rows(a2, LANES, lora_w)

    tm = min(ROW_TILE, seq)
    assert seq % tm == 0
    row = lambda n: pl.BlockSpec((tm, n), lambda i: (i, 0))
    ins = [x, ln1.reshape(1, d), w_in.astype(BF16), wl_h.astype(BF16), wl_d.astype(BF16),
           w2p.astype(BF16), a2p.astype(BF16), g2.astype(BF16), mu_rkv,
           w0.reshape(1, d_mix), a0.reshape(1, d_mix)]
    specs = [row(d), _const_spec((1, d)), _const_spec(w_in.shape), _const_spec((d, lp)),
             _const_spec((d, lp)), _const_spec((LANES, d_mix)), _const_spec((LANES, d_mix)),
             _const_spec((LANES, d_mix)), _const_spec(mu_rkv.shape),
             _const_spec((1, d_mix)), _const_spec((1, d_mix))]
    if has_vres:
        ins += [_pad_rows(v2, LANES, 0).astype(BF16), v0.reshape(1, d_mix), v_first]
        specs += [_const_spec((LANES, d_mix)), _const_spec((1, d_mix)), row(d_mix)]
    mix_out = jax.ShapeDtypeStruct((seq, d_mix), F32)
    return pl.pallas_call(
        functools.partial(_rwkv_pre_kernel, has_vres, d_mix),
        grid=(seq // tm,),
        in_specs=specs,
        out_specs=[row(d_mix)] * 6 + [row(d_memq)],
        out_shape=[mix_out] * 6 + [jax.ShapeDtypeStruct((seq, d_memq), BF16)],
        scratch_shapes=[pltpu.VMEM((1, d), F32), pltpu.VMEM((1, 3 * d_mix), F32)],
        compiler_params=_params("arbitrary"),
    )(*ins)


def _scan_chunks(chunks, kk_w, ka_w, rk_w, lnw, lnb, state, consts):
    tri_incl_b, strict, incl, eye, head_ones, same_head, head0 = consts
    n = SCAN_CHUNK
    heads = range(HEADS_PER_GROUP)
    head_mask = [head0, jnp.logical_not(head0)]
    sel = lambda xs: jnp.where(head0, xs[0], xs[1])
    nc = len(chunks)

    kk = [k * kk_w for (_, _, k, _, _) in chunks]
    ss = [_ones_dot_right(x * x, head_ones) for x in kk]
    cum = [_ones_dot_left(tri_incl_b, lw) for (_, lw, _, _, _) in chunks]

    pre = []
    for j, (r, lw, k, v, a) in enumerate(chunks):
        kkn = kk[j] / jnp.maximum(jnp.sqrt(ss[j]), 1e-12)
        kmod = k * (1.0 + (a - 1.0) * ka_w)
        c = cum[j]
        ref = c[n // 2 - 1:n // 2, :]
        inv = jnp.exp(ref - c)
        rt = r * jnp.exp(c - ref)
        at = -kkn * jnp.exp(c - lw - ref)
        pre.append(dict(
            ref=ref, e_last=jnp.exp(c[n - 1:n, :] - ref), rt=rt, at=at, kmod=kmod,
            bt=(kkn * a * inv).astype(BF16), kt=(kmod * inv).astype(BF16), vb=v.astype(BF16),
            at_h=[jnp.where(m, at, 0.0).astype(BF16) for m in head_mask],
            rt_h=[jnp.where(m, rt, 0.0).astype(BF16) for m in head_mask]))

    chains = [(j, h) for j in range(nc) for h in heads]
    ab = {(j, h): jnp.where(strict, _bdot(pre[j]["at_h"][h], pre[j]["bt"], NT), 0.0)
          for j, h in chains}
    ak = {(j, h): jnp.where(strict, _bdot(pre[j]["at_h"][h], pre[j]["kt"], NT), 0.0)
          for j, h in chains}
    rb = {(j, h): jnp.where(incl, _bdot(pre[j]["rt_h"][h], pre[j]["bt"], NT), 0.0).astype(BF16)
          for j, h in chains}
    rk = {(j, h): jnp.where(incl, _bdot(pre[j]["rt_h"][h], pre[j]["kt"], NT), 0.0).astype(BF16)
          for j, h in chains}

    t_inv = {ch: eye + ab[ch] for ch in chains}
    npow = {ch: ab[ch].astype(BF16) for ch in chains}
    for _ in range(int(math.log2(n)) - 1):
        sq = {ch: _bdot(npow[ch], npow[ch]) for ch in chains}
        npow = {ch: sq[ch].astype(BF16) for ch in chains}
        t_inv = {ch: t_inv[ch] + _bdot(t_inv[ch], npow[ch]) for ch in chains}
    t_b = {ch: t_inv[ch].astype(BF16) for ch in chains}

    akv = [sel([_bdot(ak[(j, h)], pre[j]["vb"]) for h in heads]) for j in range(nc)]
    ahat = [sel([_bdot(t_b[(j, h)], pre[j]["at"]) for h in heads]).astype(BF16) for j in range(nc)]
    uin = [sel([_bdot(t_b[(j, h)], akv[j]) for h in heads]).astype(BF16) for j in range(nc)]
    rhat = [pre[j]["rt"] + sel([_bdot(rb[(j, h)], ahat[j]) for h in heads]) for j in range(nc)]
    yin = [sel([_bdot(rb[(j, h)], uin[j]) + _bdot(rk[(j, h)], pre[j]["vb"]) for h in heads])
           for j in range(nc)]
    m_mat = [(eye + jnp.where(same_head, _bdot(ahat[j], pre[j]["bt"], TN), 0.0))
             * pre[j]["e_last"] for j in range(nc)]
    c_mat = [jnp.where(same_head, _bdot(uin[j], pre[j]["bt"], TN)
                       + _bdot(pre[j]["vb"], pre[j]["kt"], TN), 0.0) * pre[j]["e_last"]
             for j in range(nc)]
    bonus = [_ones_dot_right(chunks[j][0] * pre[j]["kmod"] * rk_w, head_ones) for j in range(nc)]

    ys = []
    for j in range(nc):
        sp = state * jnp.exp(pre[j]["ref"])
        ys.append(_bdot(rhat[j], sp, NT) + yin[j])
        state = _bdot(sp, m_mat[j]) + c_mat[j]

    inv_hd = 1.0 / HEAD_DIM
    mean = [_ones_dot_right(y, head_ones) * inv_hd for y in ys]
    dy = [y - m for y, m in zip(ys, mean)]
    var = [_ones_dot_right(d * d, head_ones) * inv_hd for d in dy]
    outs = [dy[j] * lax.rsqrt(var[j] + LN_X_EPS) * lnw + lnb + bonus[j] * chunks[j][3]
            for j in range(nc)]
    return outs, state


def _rwkv_scan_kernel(r_ref, lw_ref, k_ref, v_ref, a_ref, kk_ref, ka_ref, rk_ref,
                      lnw_ref, lnb_ref, y_ref, state_ref):
    @pl.when(pl.program_id(1) == 0)
    def _():
        state_ref[...] = jnp.zeros_like(state_ref)

    n = SCAN_CHUNK
    row = lax.broadcasted_iota(jnp.int32, (n, n), 0)
    col = lax.broadcasted_iota(jnp.int32, (n, n), 1)
    incl = row >= col
    strict = row > col
    same_head = (row // HEAD_DIM) == (col // HEAD_DIM)
    consts = (incl.astype(BF16), strict, incl, (row == col).astype(F32),
              same_head.astype(BF16), same_head, col < HEAD_DIM)

    nc = r_ref.shape[0] // n
    chunks = [tuple(ref[j * n:(j + 1) * n, :] for ref in (r_ref, lw_ref, k_ref, v_ref, a_ref))
              for j in range(nc)]
    outs, state = _scan_chunks(chunks, kk_ref[...], ka_ref[...], rk_ref[...],
                               lnw_ref[...], lnb_ref[...], state_ref[...], consts)
    for j in range(nc):
        y_ref[j * n:(j + 1) * n, :] = outs[j]
    state_ref[...] = state


def _rwkv_scan(r, lw, k, v, a, k_k, k_a, r_k, lnx_w, lnx_b):
    seq, d_mix = r.shape
    assert LANES == SCAN_CHUNK and d_mix % LANES == 0
    rows = min(SCAN_CHUNK * SCAN_CHUNKS_PER_STEP, seq)
    assert seq % rows == 0 and rows % SCAN_CHUNK == 0
    act = pl.BlockSpec((rows, LANES), lambda p, c: (c, p))
    par = pl.BlockSpec((1, LANES), lambda p, c: (0, p))
    flat = lambda t: t.reshape(1, d_mix)
    return pl.pallas_call(
        _rwkv_scan_kernel,
        grid=(d_mix // LANES, seq // rows),
        in_specs=[act] * 5 + [par] * 5,
        out_specs=act,
        out_shape=jax.ShapeDtypeStruct((seq, d_mix), F32),
        scratch_shapes=[pltpu.VMEM((LANES, LANES), F32)],
        compiler_params=_params("parallel", "arbitrary"),
    )(r, lw, k, v, a, flat(k_k), flat(k_a), flat(r_k), flat(lnx_w), flat(lnx_b))


def _norm_proj_kernel(n_proj, *refs):
    x_ref = refs[0]
    gains = refs[1:1 + n_proj]
    weights = refs[1 + n_proj:1 + 2 * n_proj]
    outs = refs[1 + 2 * n_proj:]
    xn = _rms_scale(x_ref[...])
    o = 0
    for g_ref, w_ref in zip(gains, weights):
        y = _bdot(xn * g_ref[...], w_ref[...])
        start = 0
        while start < y.shape[1]:
            width = outs[o].shape[1]
            outs[o][...] = y[:, start:start + width].astype(outs[o].dtype)
            start += width
            o += 1


def _norm_proj(x, projections):
    seq, d = x.shape
    tm = min(ROW_TILE, seq)
    row = lambda n: pl.BlockSpec((tm, n), lambda i: (i, 0))
    gains = [g.reshape(1, d) for g, _, _ in projections]
    weights = [w.astype(BF16) for _, w, _ in projections]
    widths = [n for _, _, ws in projections for n in ws]
    return pl.pallas_call(
        functools.partial(_norm_proj_kernel, len(projections)),
        grid=(seq // tm,),
        in_specs=([row(d)] + [_const_spec((1, d))] * len(gains)
                  + [_const_spec(w.shape) for w in weights]),
        out_specs=[row(n) for n in widths],
        out_shape=[jax.ShapeDtypeStruct((seq, n), BF16) for n in widths],
        compiler_params=_params("arbitrary"),
    )(x, *gains, *weights)


def _band_bias(table):
    i = jnp.arange(CHUNK)[:, None]
    j = jnp.arange(BAND)[None, :]
    idx = jnp.clip(i + LEFT_CHUNKS * CHUNK - j, -(CHUNK - 1), REL_CLIP) + (CHUNK - 1)
    return table[:, idx].astype(F32)


def _chunk_attn_kernel(q_ref, kp_ref, kc_ref, vp_ref, vc_ref, bias_ref, o_ref):
    tq = q_ref.shape[0]
    has_prev = pl.program_id(1) > 0
    k_win = jnp.concatenate([kp_ref[...], kc_ref[...]], axis=0)
    v_win = jnp.concatenate([vp_ref[...], vc_ref[...]], axis=0)
    bias = jnp.concatenate([bias_ref[hd] for hd in range(HEADS_PER_GROUP)], axis=0)
    lane = lax.broadcasted_iota(jnp.int32, (CHUNK, LANES), 1)
    col = lax.broadcasted_iota(jnp.int32, (HEADS_PER_GROUP * CHUNK, BAND), 1)
    scale = HEAD_DIM ** -0.5
    for c in range(tq // CHUNK):
        q = q_ref[c * CHUNK:(c + 1) * CHUNK, :]
        q2 = jnp.concatenate(
            [jnp.where((lane // HEAD_DIM) == hd, q, jnp.zeros_like(q))
             for hd in range(HEADS_PER_GROUP)], axis=0)
        keys = k_win[c * CHUNK:c * CHUNK + BAND, :]
        vals = v_win[c * CHUNK:c * CHUNK + BAND, :]
        s = _bdot(q2, keys, NT) * scale + bias
        s = jnp.where(jnp.logical_or(has_prev, col >= tq - c * CHUNK), s, MASK_VALUE)
        e = jnp.exp(s - jnp.max(s, axis=-1, keepdims=True))
        o2 = _bdot(e, vals) / jnp.sum(e, axis=-1, keepdims=True)
        out = o2[:CHUNK, :]
        for hd in range(1, HEADS_PER_GROUP):
            out = jnp.where((lane // HEAD_DIM) == hd, o2[hd * CHUNK:(hd + 1) * CHUNK, :], out)
        o_ref[c * CHUNK:(c + 1) * CHUNK, :] = out


def _chunk_attn(q, k, v, rel_table):
    seq, d_mix = q.shape
    tq = ATTN_TILE
    assert seq % tq == 0
    cur = pl.BlockSpec((tq, LANES), lambda p, i: (i, p))
    prev = pl.BlockSpec((tq, LANES), lambda p, i: (jnp.maximum(i - 1, 0), p))
    bias = _band_bias(rel_table)
    return pl.pallas_call(
        _chunk_attn_kernel,
        grid=(d_mix // LANES, seq // tq),
        in_specs=[cur, prev, cur, prev, cur,
                  pl.BlockSpec((HEADS_PER_GROUP, CHUNK, BAND), lambda p, i: (p, 0, 0))],
        out_specs=cur,
        out_shape=jax.ShapeDtypeStruct((seq, d_mix), F32),
        compiler_params=_params("parallel", "arbitrary"),
    )(q, k, k, v, v, bias)


def _layer_post_kernel(has_gate, has_final, d_mix, d_ff, *refs):
    refs = list(refs)
    x_ref, mix_ref = refs[:2]
    refs = refs[2:]
    gate_ref = refs.pop(0) if has_gate else None
    (qm_ref, kvm_ref, wout_ref, ln2_ref, fin_ref, cw_ref, cb_ref, fout_ref) = refs[:8]
    refs = refs[8:]
    lnf_ref = refs.pop(0) if has_final else None
    out_ref, carry = refs

    @pl.when(pl.program_id(0) == 0)
    def _():
        carry[...] = jnp.zeros_like(carry)

    tm = x_ref.shape[0]
    d_memq = qm_ref.shape[1]
    mix = mix_ref[...]
    if has_gate:
        mix = mix * gate_ref[...]

    qm = qm_ref[...]
    km = kvm_ref[:, :d_memq]
    vm = kvm_ref[:, d_memq:]
    lane = lax.broadcasted_iota(jnp.int32, qm.shape, 1)
    mo = None
    for hd in range(d_memq // HEAD_DIM):
        mh = (lane // HEAD_DIM) == hd
        s = _bdot(jnp.where(mh, qm, jnp.zeros_like(qm)), km, NT) * HEAD_DIM ** -0.5
        e = jnp.exp(s - jnp.max(s, axis=-1, keepdims=True))
        o_h = _bdot(e, vm) / jnp.sum(e, axis=-1, keepdims=True)
        mo = o_h if mo is None else jnp.where(mh, o_h, mo)

    x1 = (x_ref[...] + _bdot(mix, wout_ref[:d_mix, :]) + _bdot(mo, wout_ref[d_mix:, :]))

    h2 = (_rms_scale(x1) * ln2_ref[...]).astype(BF16)
    ffn = None
    for c0 in range(0, d_ff, FFN_COLS):
        cols = slice(c0, c0 + FFN_COLS)
        gate = jnp.dot(h2, fin_ref[:, cols], preferred_element_type=F32)
        val = jnp.dot(h2, fin_ref[:, d_ff + c0:d_ff + c0 + FFN_COLS], preferred_element_type=F32)
        prev = carry[SUBLANES - (CONV_W - 1):, cols]
        carry[:, cols] = gate[tm - SUBLANES:, :]
        conv = cb_ref[:, cols] + cw_ref[CONV_W - 1:CONV_W, cols] * gate
        for j in range(1, CONV_W):
            conv = conv + (cw_ref[CONV_W - 1 - j:CONV_W - j, cols]
                           * _shift_rows(gate, prev[CONV_W - 1 - j:, :], j))
        act = 0.5 * conv * (1.0 + lax.erf(conv * (2.0 ** -0.5))) * val
        slab = _bdot(act, fout_ref[cols, :])
        ffn = slab if ffn is None else ffn + slab
    acc = x1 + ffn
    if has_final:
        acc = _rms_scale(acc) * lnf_ref[...]
    out_ref[...] = acc


def _layer_post(x, mix, gate, qm, kv_mem, w_out, ln2, ffn_in, conv_w, conv_b, ffn_out, ln_f):
    seq, d = x.shape
    d_mix = mix.shape[1]
    d_memq = qm.shape[1]
    d_ff = ffn_out.shape[0]
    assert d_ff % FFN_COLS == 0 and conv_w.shape[0] == CONV_W
    tm = min(ROW_TILE, seq)
    row = lambda n: pl.BlockSpec((tm, n), lambda i: (i, 0))
    ins, specs = [x, mix], [row(d), row(d_mix)]
    if gate is not None:
        ins.append(gate)
        specs.append(row(d_mix))
    ins += [qm, kv_mem, w_out.astype(BF16), ln2.reshape(1, d), ffn_in.astype(BF16),
            conv_w, conv_b.reshape(1, d_ff), ffn_out.astype(BF16)]
    specs += [row(d_memq), _const_spec(kv_mem.shape), _const_spec(w_out.shape),
              _const_spec((1, d)), _const_spec(ffn_in.shape), _const_spec(conv_w.shape),
              _const_spec((1, d_ff)), _const_spec(ffn_out.shape)]
    if ln_f is not None:
        ins.append(ln_f.reshape(1, d))
        specs.append(_const_spec((1, d)))
    return pl.pallas_call(
        functools.partial(_layer_post_kernel, gate is not None, ln_f is not None, d_mix, d_ff),
        grid=(seq // tm,),
        in_specs=specs,
        out_specs=row(d),
        out_shape=jax.ShapeDtypeStruct((seq, d), F32),
        scratch_shapes=[pltpu.VMEM((SUBLANES, d_ff), F32)],
        compiler_params=_params("arbitrary"),
    )(*ins)


def kernel(x, mem, mem_norm, ln1, ln2, w_out, w_mem_kv, ffn_in, ffn_conv, ffn_conv_b, ffn_out, a_w_in, a_mu_rkv, a_mu_x, a_w0, a_w1, a_w2, a_a0, a_a1, a_a2, a_g1, a_g2, a_k_k, a_k_a, a_r_k, a_lnx_w, a_lnx_b, a_mu_v, a_v0, a_v1, a_v2, ln_kv, w_kv, b_w_in, b_rel, ln_f):
    bsz, seq, d = x.shape
    assert bsz == 1 and mem.shape[0] == 1
    depth = ln1.shape[0]
    n_a = a_w_in.shape[0]
    d_mix = a_w0.shape[1]
    xs = x.reshape(seq, d)
    kv_mem = _mem_kv(mem.reshape(mem.shape[1], d), mem_norm, w_mem_kv)

    v_first = None
    k_s = v_s = None
    for layer in range(depth):
        last = ln_f if layer == depth - 1 else None
        if layer < n_a:
            i = layer
            vres = None if i == 0 else (a_mu_v[i - 1], a_v0[i - 1], a_v1[i - 1], a_v2[i - 1])
            r, lw, k, v, a, g, qm = _rwkv_pre(
                xs, ln1[layer], a_w_in[i], a_mu_rkv[i], a_mu_x[i], a_w0[i], a_w1[i], a_w2[i],
                a_a0[i], a_a1[i], a_a2[i], a_g1[i], a_g2[i], vres, v_first)
            if i == 0:
                v_first = v
            mix = _rwkv_scan(r, lw, k, v, a, a_k_k[i], a_k_a[i], a_r_k[i],
                             a_lnx_w[i], a_lnx_b[i])
            gate = g
        else:
            j = layer - n_a
            d_memq = b_w_in.shape[2] - d_mix
            projections = [(ln1[layer], b_w_in[j], [d_mix, d_memq])]
            if j == 0:
                projections.append((ln_kv, w_kv, [d_mix, d_mix]))
            outs = _norm_proj(xs, projections)
            q, qm = outs[:2]
            if j == 0:
                k_s, v_s = outs[2:]
            mix = _chunk_attn(q, k_s, v_s, b_rel[j])
            gate = None
        xs = _layer_post(xs, mix, gate, qm, kv_mem[layer], w_out[layer], ln2[layer],
                         ffn_in[layer], ffn_conv[layer], ffn_conv_b[layer], ffn_out[layer], last)
    return xs.reshape(bsz, seq, d)
```

```python
import functools
import math

import jax
import jax.numpy as jnp
from jax import lax
from jax.experimental import pallas as pl
from jax.experimental.pallas import tpu as pltpu

F32 = jnp.float32
BF16 = jnp.bfloat16

HEAD_DIM = 64
LANES = 128
HEADS_PER_GROUP = LANES // HEAD_DIM
SUBLANES = 8
N_MEM_HEADS = 4
CHUNK = 64
LEFT_CHUNKS = 8
BAND = (LEFT_CHUNKS + 1) * CHUNK
REL_CLIP = 256
CONV_W = 3
LN_X_EPS = 64e-5
RMS_EPS = 1e-6
MASK_VALUE = -1e30

SCAN_CHUNK = 128
SCAN_CHUNKS_PER_STEP = 4
ROW_TILE = 512
ATTN_TILE = LEFT_CHUNKS * CHUNK
FFN_COLS = 256
VMEM_LIMIT = 56 * 1024 * 1024

NT = (((1,), (1,)), ((), ()))
TN = (((0,), (0,)), ((), ()))
NN = (((1,), (0,)), ((), ()))


def _bdot(a, b, dims=NN):
    return lax.dot_general(a.astype(BF16), b.astype(BF16), dims,
                           preferred_element_type=F32)


def _split(x):
    hi = x.astype(BF16)
    lo = (x - hi.astype(F32)).astype(BF16)
    return hi, lo


def _ones_dot_right(x, ones_bf16):
    hi, lo = _split(x)
    return (jnp.dot(hi, ones_bf16, preferred_element_type=F32)
            + jnp.dot(lo, ones_bf16, preferred_element_type=F32))


def _ones_dot_left(ones_bf16, x):
    hi, lo = _split(x)
    return (jnp.dot(ones_bf16, hi, preferred_element_type=F32)
            + jnp.dot(ones_bf16, lo, preferred_element_type=F32))


def _rms_scale(x):
    return x * lax.rsqrt(jnp.mean(x * x, axis=-1, keepdims=True) + RMS_EPS)


def _shift_rows(t, prev_rows, n):
    rolled = pltpu.roll(t, n, 0)
    row = lax.broadcasted_iota(jnp.int32, t.shape, 0)
    out = rolled
    for j in range(n):
        out = jnp.where(row == j, prev_rows[j:j + 1, :], out)
    return out


def _const_spec(shape):
    nd = len(shape)
    return pl.BlockSpec(shape, lambda *_: (0,) * nd, pipeline_mode=pl.Buffered(1))


def _params(*sem):
    return pltpu.CompilerParams(dimension_semantics=sem, vmem_limit_bytes=VMEM_LIMIT)


def _mem_kv_kernel(mem_ref, g_ref, w_ref, out_ref):
    mem_n = _rms_scale(mem_ref[...]) * g_ref[...]
    out_ref[...] = _bdot(mem_n, w_ref[...]).astype(out_ref.dtype)


def _mem_kv(mem, mem_norm, w_mem_kv):
    depth, d, n = w_mem_kv.shape
    n_mem = mem.shape[0]
    return pl.pallas_call(
        _mem_kv_kernel,
        grid=(depth,),
        in_specs=[pl.BlockSpec((n_mem, d), lambda l: (0, 0)),
                  pl.BlockSpec((1, d), lambda l: (0, 0)),
                  pl.BlockSpec((None, d, n), lambda l: (l, 0, 0))],
        out_specs=pl.BlockSpec((None, n_mem, n), lambda l: (l, 0, 0)),
        out_shape=jax.ShapeDtypeStruct((depth, n_mem, n), BF16),
        compiler_params=_params("arbitrary"),
    )(mem, mem_norm.reshape(1, d), w_mem_kv.astype(BF16))


def _softplus(u):
    return jnp.maximum(u, 0.0) + jnp.log1p(jnp.exp(-jnp.abs(u)))


def _rwkv_pre_kernel(has_vres, d_mix, *refs):
    if has_vres:
        (x_ref, ln_ref, win_ref, wlh_ref, wld_ref, w2_ref, a2_ref, g2_ref, mu_ref,
         w0_ref, a0_ref, v2_ref, v0_ref, vf_ref,
         r_ref, lw_ref, k_ref, v_ref, a_ref, g_ref, qm_ref, hprev, pprev) = refs
    else:
        (x_ref, ln_ref, win_ref, wlh_ref, wld_ref, w2_ref, a2_ref, g2_ref, mu_ref,
         w0_ref, a0_ref,
         r_ref, lw_ref, k_ref, v_ref, a_ref, g_ref, qm_ref, hprev, pprev) = refs

    @pl.when(pl.program_id(0) == 0)
    def _():
        hprev[...] = jnp.zeros_like(hprev)
        pprev[...] = jnp.zeros_like(pprev)

    tm = x_ref.shape[0]
    h = _rms_scale(x_ref[...]) * ln_ref[...]
    dh = _shift_rows(h, hprev[...], 1) - h
    hprev[...] = h[tm - 1:tm, :]
    hb = h.astype(BF16)
    p = jnp.dot(hb, win_ref[...], preferred_element_type=F32)
    l1 = (jnp.dot(hb, wlh_ref[...], preferred_element_type=F32)
          + jnp.dot(dh.astype(BF16), wld_ref[...], preferred_element_type=F32))

    prkv = p[:, :3 * d_mix]
    ps = _shift_rows(prkv, pprev[...], 1)
    pprev[...] = prkv[tm - 1:tm, :]
    mu = mu_ref[...]

    def lerp(j):
        cur = prkv[:, j * d_mix:(j + 1) * d_mix]
        return cur + (ps[:, j * d_mix:(j + 1) * d_mix] - cur) * mu[j:j + 1, :]

    r_ref[...] = lerp(0)
    k_ref[...] = lerp(1)
    v = lerp(2)
    qm_ref[...] = p[:, 3 * d_mix:].astype(qm_ref.dtype)

    l_wa = l1[:, :LANES]
    z = w0_ref[...] + _bdot(jnp.tanh(l_wa), w2_ref[...])
    w_log = -_softplus(-z) - 0.5
    lw_ref[...] = -jnp.exp(w_log)
    a_ref[...] = jax.nn.sigmoid(a0_ref[...] + _bdot(l_wa, a2_ref[...]))
    g_ref[...] = _bdot(jax.nn.sigmoid(l1[:, LANES:2 * LANES]), g2_ref[...])
    if has_vres:
        gate = jax.nn.sigmoid(v0_ref[...] + _bdot(l1[:, 2 * LANES:3 * LANES], v2_ref[...]))
        v = v + (vf_ref[...] - v) * gate
    v_ref[...] = v


def _pad_rows(w, rows, offset):
    out = jnp.zeros((rows, w.shape[1]), w.dtype)
    return out.at[offset:offset + w.shape[0]].set(w)


def _rwkv_pre(x, ln1, w_in, mu_rkv, mu_x, w0, w1, w2, a0, a1, a2, g1, g2, vres, v_first):
    seq, d = x.shape
    d_mix = w0.shape[0]
    d_memq = w_in.shape[1] - 3 * d_mix
    lora_w, lora_a, lora_g = w1.shape[1], a1.shape[1], g1.shape[1]
    assert lora_w + lora_a == LANES and lora_g == LANES
    has_vres = vres is not None
    firsts = [w1, a1, g1]
    mus = [mu_x[0], mu_x[1], mu_x[2]]
    if has_vres:
        mu_v, v0, v1, v2 = vres
        assert v1.shape[1] <= LANES
        firsts.append(jnp.pad(v1, ((0, 0), (0, LANES - v1.shape[1]))))
        mus.append(mu_v)
    wl_h = jnp.concatenate(firsts, axis=1)
    wl_d = jnp.concatenate([m[:, None] * w for m, w in zip(mus, firsts)], axis=1)
    lp = wl_h.shape[1]
    w2p = _pad_rows(w2, LANES, 0)
    a2p = _pad_rows(a2, LANES, lora_w)

    tm = min(ROW_TILE, seq)
    assert seq % tm == 0
    row = lambda n: pl.BlockSpec((tm, n), lambda i: (i, 0))
    ins = [x, ln1.reshape(1, d), w_in.astype(BF16), wl_h.astype(BF16), wl_d.astype(BF16),
           w2p.astype(BF16), a2p.astype(BF16), g2.astype(BF16), mu_rkv,
           w0.reshape(1, d_mix), a0.reshape(1, d_mix)]
    specs = [row(d), _const_spec((1, d)), _const_spec(w_in.shape), _const_spec((d, lp)),
             _const_spec((d, lp)), _const_spec((LANES, d_mix)), _const_spec((LANES, d_mix)),
             _const_spec((LANES, d_mix)), _const_spec(mu_rkv.shape),
             _const_spec((1, d_mix)), _const_spec((1, d_mix))]
    if has_vres:
        ins += [_pad_rows(v2, LANES, 0).astype(BF16), v0.reshape(1, d_mix), v_first]
        specs += [_const_spec((LANES, d_mix)), _const_spec((1, d_mix)), row(d_mix)]
    mix_out = jax.ShapeDtypeStruct((seq, d_mix), F32)
    return pl.pallas_call(
        functools.partial(_rwkv_pre_kernel, has_vres, d_mix),
        grid=(seq // tm,),
        in_specs=specs,
        out_specs=[row(d_mix)] * 6 + [row(d_memq)],
        out_shape=[mix_out] * 6 + [jax.ShapeDtypeStruct((seq, d_memq), BF16)],
        scratch_shapes=[pltpu.VMEM((1, d), F32), pltpu.VMEM((1, 3 * d_mix), F32)],
        compiler_params=_params("arbitrary"),
    )(*ins)


def _scan_chunks(chunks, kk_w, ka_w, rk_w, lnw, lnb, state, consts):
    tri_incl_b, strict, incl, eye, head_ones, same_head, head0 = consts
    n = SCAN_CHUNK
    heads = range(HEADS_PER_GROUP)
    head_mask = [head0, jnp.logical_not(head0)]
    sel = lambda xs: jnp.where(head0, xs[0], xs[1])
    nc = len(chunks)

    def stacked_head_sums(xs):
        parts = [p for x in xs for p in _split(x)]
        tot = jnp.dot(jnp.concatenate(parts, axis=0), head_ones, preferred_element_type=F32)
        return [tot[(2 * i) * n:(2 * i + 1) * n] + tot[(2 * i + 1) * n:(2 * i + 2) * n]
                for i in range(len(xs))]

    kk = [k * kk_w for (_, _, k, _, _) in chunks]
    kmod = [k * (1.0 + (a - 1.0) * ka_w) for (_, _, k, _, a) in chunks]
    sums = stacked_head_sums([x * x for x in kk]
                             + [chunks[j][0] * kmod[j] * rk_w for j in range(nc)])
    ss, bonus = sums[:nc], sums[nc:]
    cum = []
    for (_, lw, _, _, _) in chunks:
        hi, lo = _split(lw)
        both = jnp.dot(tri_incl_b, jnp.concatenate([hi, lo], axis=1), preferred_element_type=F32)
        cum.append(both[:, :LANES] + both[:, LANES:])

    pre = []
    for j, (r, lw, k, v, a) in enumerate(chunks):
        kkn = kk[j] / jnp.maximum(jnp.sqrt(ss[j]), 1e-12)
        c = cum[j]
        ref = c[n // 2 - 1:n // 2, :]
        inv = jnp.exp(ref - c)
        rt = r * jnp.exp(c - ref)
        at = -kkn * jnp.exp(c - lw - ref)
        lhs = [jnp.where(m, x, 0.0).astype(BF16) for m in head_mask for x in (at, rt)]
        pre.append(dict(
            ref=ref, e_last=jnp.exp(c[n - 1:n, :] - ref), rt=rt, at=at.astype(BF16),
            vb=v.astype(BF16), lhs=jnp.concatenate(lhs, axis=0),
            bk=jnp.concatenate([(kkn * a * inv).astype(BF16), (kmod[j] * inv).astype(BF16)],
                               axis=0)))

    quad = [_bdot(pre[j]["lhs"], pre[j]["bk"], NT) for j in range(nc)]
    chains = [(j, h) for j in range(nc) for h in heads]
    strict2 = jnp.concatenate([strict, strict], axis=1)
    incl2 = jnp.concatenate([incl, incl], axis=1)
    ab_ak = {(j, h): jnp.where(strict2, quad[j][2 * h * n:(2 * h + 1) * n], 0.0)
             for j, h in chains}
    rb_rk = {(j, h): jnp.where(incl2, quad[j][(2 * h + 1) * n:(2 * h + 2) * n], 0.0).astype(BF16)
             for j, h in chains}

    t_inv = {ch: eye + ab_ak[ch][:, :n] for ch in chains}
    nb = {ch: ab_ak[ch][:, :n].astype(BF16) for ch in chains}
    power = {ch: _bdot(nb[ch], nb[ch]).astype(BF16) for ch in chains}
    for _ in range(int(math.log2(n)) - 2):
        both = {ch: _bdot(jnp.concatenate([t_inv[ch].astype(BF16), power[ch]], axis=0), power[ch])
                for ch in chains}
        t_inv = {ch: t_inv[ch] + both[ch][:n] for ch in chains}
        power = {ch: both[ch][n:].astype(BF16) for ch in chains}
    t_b = {ch: (t_inv[ch] + _bdot(t_inv[ch], power[ch])).astype(BF16) for ch in chains}

    akv = [sel([_bdot(ab_ak[(j, h)][:, n:], pre[j]["vb"]) for h in heads]).astype(BF16)
           for j in range(nc)]
    t_x = [[_bdot(t_b[(j, h)], jnp.concatenate([pre[j]["at"], akv[j]], axis=1)) for h in heads]
           for j in range(nc)]
    ahat = [sel([t[:, :LANES] for t in t_x[j]]).astype(BF16) for j in range(nc)]
    uin = [sel([t[:, LANES:] for t in t_x[j]]).astype(BF16) for j in range(nc)]
    zero = jnp.zeros((n, LANES), BF16)
    r_x = [[_bdot(rb_rk[(j, h)],
                  jnp.concatenate([jnp.concatenate([ahat[j], uin[j]], axis=1),
                                   jnp.concatenate([zero, pre[j]["vb"]], axis=1)], axis=0))
            for h in heads] for j in range(nc)]
    rhat = [pre[j]["rt"] + sel([t[:, :LANES] for t in r_x[j]]) for j in range(nc)]
    yin = [sel([t[:, LANES:] for t in r_x[j]]) for j in range(nc)]
    m_mat = [(eye + jnp.where(same_head, _bdot(ahat[j], pre[j]["bk"][:n], TN), 0.0))
             * pre[j]["e_last"] for j in range(nc)]
    c_mat = [jnp.where(same_head,
                       _bdot(jnp.concatenate([uin[j], pre[j]["vb"]], axis=0), pre[j]["bk"], TN),
                       0.0) * pre[j]["e_last"] for j in range(nc)]

    ys = []
    for j in range(nc):
        sp = state * jnp.exp(pre[j]["ref"])
        ys.append(_bdot(rhat[j], sp, NT) + yin[j])
        state = _bdot(sp, m_mat[j]) + c_mat[j]

    inv_hd = 1.0 / HEAD_DIM
    mean = [m * inv_hd for m in stacked_head_sums(ys)]
    dy = [y - m for y, m in zip(ys, mean)]
    var = [s * inv_hd for s in stacked_head_sums([d * d for d in dy])]
    outs = [dy[j] * lax.rsqrt(var[j] + LN_X_EPS) * lnw + lnb + bonus[j] * chunks[j][3]
            for j in range(nc)]
    return outs, state


def _rwkv_scan_kernel(r_ref, lw_ref, k_ref, v_ref, a_ref, kk_ref, ka_ref, rk_ref,
                      lnw_ref, lnb_ref, y_ref, state_ref):
    @pl.when(pl.program_id(1) == 0)
    def _():
        state_ref[...] = jnp.zeros_like(state_ref)

    n = SCAN_CHUNK
    row = lax.broadcasted_iota(jnp.int32, (n, n), 0)
    col = lax.broadcasted_iota(jnp.int32, (n, n), 1)
    incl = row >= col
    strict = row > col
    same_head = (row // HEAD_DIM) == (col // HEAD_DIM)
    consts = (incl.astype(BF16), strict, incl, (row == col).astype(F32),
              same_head.astype(BF16), same_head, col < HEAD_DIM)

    nc = r_ref.shape[0] // n
    chunks = [tuple(ref[j * n:(j + 1) * n, :] for ref in (r_ref, lw_ref, k_ref, v_ref, a_ref))
              for j in range(nc)]
    outs, state = _scan_chunks(chunks, kk_ref[...], ka_ref[...], rk_ref[...],
                               lnw_ref[...], lnb_ref[...], state_ref[...], consts)
    for j in range(nc):
        y_ref[j * n:(j + 1) * n, :] = outs[j]
    state_ref[...] = state


def _rwkv_scan(r, lw, k, v, a, k_k, k_a, r_k, lnx_w, lnx_b):
    seq, d_mix = r.shape
    assert LANES == SCAN_CHUNK and d_mix % LANES == 0
    rows = min(SCAN_CHUNK * SCAN_CHUNKS_PER_STEP, seq)
    assert seq % rows == 0 and rows % SCAN_CHUNK == 0
    act = pl.BlockSpec((rows, LANES), lambda p, c: (c, p))
    par = pl.BlockSpec((1, LANES), lambda p, c: (0, p))
    flat = lambda t: t.reshape(1, d_mix)
    return pl.pallas_call(
        _rwkv_scan_kernel,
        grid=(d_mix // LANES, seq // rows),
        in_specs=[act] * 5 + [par] * 5,
        out_specs=act,
        out_shape=jax.ShapeDtypeStruct((seq, d_mix), F32),
        scratch_shapes=[pltpu.VMEM((LANES, LANES), F32)],
        compiler_params=_params("parallel", "arbitrary"),
    )(r, lw, k, v, a, flat(k_k), flat(k_a), flat(r_k), flat(lnx_w), flat(lnx_b))


def _norm_proj_kernel(n_proj, *refs):
    x_ref = refs[0]
    gains = refs[1:1 + n_proj]
    weights = refs[1 + n_proj:1 + 2 * n_proj]
    outs = refs[1 + 2 * n_proj:]
    xn = _rms_scale(x_ref[...])
    o = 0
    for g_ref, w_ref in zip(gains, weights):
        y = _bdot(xn * g_ref[...], w_ref[...])
        start = 0
        while start < y.shape[1]:
            width = outs[o].shape[1]
            outs[o][...] = y[:, start:start + width].astype(outs[o].dtype)
            start += width
            o += 1


def _norm_proj(x, projections):
    seq, d = x.shape
    tm = min(ROW_TILE, seq)
    row = lambda n: pl.BlockSpec((tm, n), lambda i: (i, 0))
    gains = [g.reshape(1, d) for g, _, _ in projections]
    weights = [w.astype(BF16) for _, w, _ in projections]
    widths = [n for _, _, ws in projections for n in ws]
    return pl.pallas_call(
        functools.partial(_norm_proj_kernel, len(projections)),
        grid=(seq // tm,),
        in_specs=([row(d)] + [_const_spec((1, d))] * len(gains)
                  + [_const_spec(w.shape) for w in weights]),
        out_specs=[row(n) for n in widths],
        out_shape=[jax.ShapeDtypeStruct((seq, n), BF16) for n in widths],
        compiler_params=_params("arbitrary"),
    )(x, *gains, *weights)


def _band_bias(table):
    ext = jnp.concatenate(
        [table, jnp.broadcast_to(table[:, -1:], (table.shape[0], BAND + CHUNK - 1 - table.shape[1]))],
        axis=1).astype(F32)
    rows = [ext[:, i:i + BAND] for i in range(CHUNK)]
    return jnp.stack(rows, axis=1)[:, :, ::-1]


def _chunk_attn_tile(first_tile, q_ref, kp_ref, kc_ref, vp_ref, vc_ref, bias_ref, o_ref):
    tq = q_ref.shape[0]
    k_win = jnp.concatenate([kp_ref[...], kc_ref[...]], axis=0)
    v_win = jnp.concatenate([vp_ref[...], vc_ref[...]], axis=0)
    bias = jnp.concatenate([bias_ref[hd] for hd in range(HEADS_PER_GROUP)], axis=0)
    lane = lax.broadcasted_iota(jnp.int32, (CHUNK, LANES), 1)
    col = lax.broadcasted_iota(jnp.int32, (HEADS_PER_GROUP * CHUNK, BAND), 1)
    scale = HEAD_DIM ** -0.5
    blocks = range(tq // CHUNK)
    scores = []
    for c in blocks:
        q = q_ref[c * CHUNK:(c + 1) * CHUNK, :] * scale
        q2 = jnp.concatenate(
            [jnp.where((lane // HEAD_DIM) == hd, q, jnp.zeros_like(q))
             for hd in range(HEADS_PER_GROUP)], axis=0)
        scores.append(_bdot(q2, k_win[c * CHUNK:c * CHUNK + BAND, :], NT))
    probs, denoms = [], []
    for c in blocks:
        s = scores[c] + bias
        if first_tile:
            s = jnp.where(col >= tq - c * CHUNK, s, MASK_VALUE)
        e = jnp.exp(s - jnp.max(s, axis=-1, keepdims=True))
        denoms.append(jnp.sum(e, axis=-1, keepdims=True))
        probs.append(e.astype(BF16))
    for c in blocks:
        o2 = _bdot(probs[c], v_win[c * CHUNK:c * CHUNK + BAND, :]) / denoms[c]
        out = o2[:CHUNK, :]
        for hd in range(1, HEADS_PER_GROUP):
            out = jnp.where((lane // HEAD_DIM) == hd, o2[hd * CHUNK:(hd + 1) * CHUNK, :], out)
        o_ref[c * CHUNK:(c + 1) * CHUNK, :] = out


def _chunk_attn_kernel(*refs):
    is_first = pl.program_id(1) == 0
    pl.when(is_first)(functools.partial(_chunk_attn_tile, True, *refs))
    pl.when(jnp.logical_not(is_first))(functools.partial(_chunk_attn_tile, False, *refs))


def _chunk_attn(q, k, v, rel_table):
    seq, d_mix = q.shape
    tq = ATTN_TILE
    assert seq % tq == 0
    cur = pl.BlockSpec((tq, LANES), lambda p, i: (i, p))
    prev = pl.BlockSpec((tq, LANES), lambda p, i: (jnp.maximum(i - 1, 0), p))
    bias = _band_bias(rel_table)
    return pl.pallas_call(
        _chunk_attn_kernel,
        grid=(d_mix // LANES, seq // tq),
        in_specs=[cur, prev, cur, prev, cur,
                  pl.BlockSpec((HEADS_PER_GROUP, CHUNK, BAND), lambda p, i: (p, 0, 0))],
        out_specs=cur,
        out_shape=jax.ShapeDtypeStruct((seq, d_mix), F32),
        compiler_params=_params("parallel", "arbitrary"),
    )(q, k, k, v, v, bias)


def _layer_post_kernel(has_gate, has_final, d_mix, d_ff, *refs):
    refs = list(refs)
    x_ref, mix_ref = refs[:2]
    refs = refs[2:]
    gate_ref = refs.pop(0) if has_gate else None
    (qm_ref, kvm_ref, wout_ref, ln2_ref, fin_ref, cw_ref, cb_ref, fout_ref) = refs[:8]
    refs = refs[8:]
    lnf_ref = refs.pop(0) if has_final else None
    out_ref, carry = refs

    @pl.when(pl.program_id(0) == 0)
    def _():
        carry[...] = jnp.zeros_like(carry)

    tm = x_ref.shape[0]
    d_memq = qm_ref.shape[1]
    mix = mix_ref[...]
    if has_gate:
        mix = mix * gate_ref[...]

    proj_mix = _bdot(mix, wout_ref[:d_mix, :])

    qm = qm_ref[...] * HEAD_DIM ** -0.5
    km = kvm_ref[:, :d_memq]
    vm = kvm_ref[:, d_memq:]
    lane = lax.broadcasted_iota(jnp.int32, qm.shape, 1)
    head_masks = [(lane // HEAD_DIM) == hd for hd in range(d_memq // HEAD_DIM)]
    scores = [_bdot(jnp.where(mh, qm, jnp.zeros_like(qm)), km, NT) for mh in head_masks]
    exps = [jnp.exp(s - jnp.max(s, axis=-1, keepdims=True)) for s in scores]
    heads_out = [_bdot(e, vm) / jnp.sum(e, axis=-1, keepdims=True) for e in exps]
    mo = heads_out[0]
    for mh, o_h in zip(head_masks[1:], heads_out[1:]):
        mo = jnp.where(mh, o_h, mo)
    x1 = x_ref[...] + (proj_mix + _bdot(mo, wout_ref[d_mix:, :]))

    h2 = (_rms_scale(x1) * ln2_ref[...]).astype(BF16)

    def in_dots(c0):
        return (jnp.dot(h2, fin_ref[:, c0:c0 + FFN_COLS], preferred_element_type=F32),
                jnp.dot(h2, fin_ref[:, d_ff + c0:d_ff + c0 + FFN_COLS],
                        preferred_element_type=F32))

    slabs = list(range(0, d_ff, FFN_COLS))
    ffn = None
    nxt = in_dots(slabs[0])
    for idx, c0 in enumerate(slabs):
        cols = slice(c0, c0 + FFN_COLS)
        gate, val = nxt
        if idx + 1 < len(slabs):
            nxt = in_dots(slabs[idx + 1])
        prev = carry[SUBLANES - (CONV_W - 1):, cols]
        carry[:, cols] = gate[tm - SUBLANES:, :]
        conv = cb_ref[:, cols] + cw_ref[CONV_W - 1:CONV_W, cols] * gate
        for j in range(1, CONV_W):
            conv = conv + (cw_ref[CONV_W - 1 - j:CONV_W - j, cols]
                           * _shift_rows(gate, prev[CONV_W - 1 - j:, :], j))
        act = 0.5 * conv * (1.0 + lax.erf(conv * (2.0 ** -0.5))) * val
        slab = _bdot(act, fout_ref[cols, :])
        ffn = slab if ffn is None else ffn + slab
    acc = x1 + ffn
    if has_final:
        acc = _rms_scale(acc) * lnf_ref[...]
    out_ref[...] = acc


def _layer_post(x, mix, gate, qm, kv_mem, w_out, ln2, ffn_in, conv_w, conv_b, ffn_out, ln_f):
    seq, d = x.shape
    d_mix = mix.shape[1]
    d_memq = qm.shape[1]
    d_ff = ffn_out.shape[0]
    assert d_ff % FFN_COLS == 0 and conv_w.shape[0] == CONV_W
    tm = min(ROW_TILE, seq)
    row = lambda n: pl.BlockSpec((tm, n), lambda i: (i, 0))
    ins, specs = [x, mix], [row(d), row(d_mix)]
    if gate is not None:
        ins.append(gate)
        specs.append(row(d_mix))
    ins += [qm, kv_mem, w_out.astype(BF16), ln2.reshape(1, d), ffn_in.astype(BF16),
            conv_w, conv_b.reshape(1, d_ff), ffn_out.astype(BF16)]
    specs += [row(d_memq), _const_spec(kv_mem.shape), _const_spec(w_out.shape),
              _const_spec((1, d)), _const_spec(ffn_in.shape), _const_spec(conv_w.shape),
              _const_spec((1, d_ff)), _const_spec(ffn_out.shape)]
    if ln_f is not None:
        ins.append(ln_f.reshape(1, d))
        specs.append(_const_spec((1, d)))
    return pl.pallas_call(
        functools.partial(_layer_post_kernel, gate is not None, ln_f is not None, d_mix, d_ff),
        grid=(seq // tm,),
        in_specs=specs,
        out_specs=row(d),
        out_shape=jax.ShapeDtypeStruct((seq, d), F32),
        scratch_shapes=[pltpu.VMEM((SUBLANES, d_ff), F32)],
        compiler_params=_params("arbitrary"),
    )(*ins)


def kernel(x, mem, mem_norm, ln1, ln2, w_out, w_mem_kv, ffn_in, ffn_conv, ffn_conv_b, ffn_out, a_w_in, a_mu_rkv, a_mu_x, a_w0, a_w1, a_w2, a_a0, a_a1, a_a2, a_g1, a_g2, a_k_k, a_k_a, a_r_k, a_lnx_w, a_lnx_b, a_mu_v, a_v0, a_v1, a_v2, ln_kv, w_kv, b_w_in, b_rel, ln_f):
    bsz, seq, d = x.shape
    assert bsz == 1 and mem.shape[0] == 1
    depth = ln1.shape[0]
    n_a = a_w_in.shape[0]
    d_mix = a_w0.shape[1]
    xs = x.reshape(seq, d)
    kv_mem = _mem_kv(mem.reshape(mem.shape[1], d), mem_norm, w_mem_kv)

    v_first = None
    k_s = v_s = None
    for layer in range(depth):
        last = ln_f if layer == depth - 1 else None
        if layer < n_a:
            i = layer
            vres = None if i == 0 else (a_mu_v[i - 1], a_v0[i - 1], a_v1[i - 1], a_v2[i - 1])
            r, lw, k, v, a, g, qm = _rwkv_pre(
                xs, ln1[layer], a_w_in[i], a_mu_rkv[i], a_mu_x[i], a_w0[i], a_w1[i], a_w2[i],
                a_a0[i], a_a1[i], a_a2[i], a_g1[i], a_g2[i], vres, v_first)
            if i == 0:
                v_first = v
            mix = _rwkv_scan(r, lw, k, v, a, a_k_k[i], a_k_a[i], a_r_k[i],
                             a_lnx_w[i], a_lnx_b[i])
            gate = g
        else:
            j = layer - n_a
            d_memq = b_w_in.shape[2] - d_mix
            projections = [(ln1[layer], b_w_in[j], [d_mix, d_memq])]
            if j == 0:
                projections.append((ln_kv, w_kv, [d_mix, d_mix]))
            outs = _norm_proj(xs, projections)
            q, qm = outs[:2]
            if j == 0:
                k_s, v_s = outs[2:]
            mix = _chunk_attn(q, k_s, v_s, b_rel[j])
            gate = None
        xs = _layer_post(xs, mix, gate, qm, kv_mem[layer], w_out[layer], ln2[layer],
                         ffn_in[layer], ffn_conv[layer], ffn_conv_b[layer], ffn_out[layer], last)
    return xs.reshape(bsz, seq, d)
```

```python
import functools
import math

import jax
import jax.numpy as jnp
from jax import lax
from jax.experimental import pallas as pl
from jax.experimental.pallas import tpu as pltpu

F32 = jnp.float32
BF16 = jnp.bfloat16

HEAD_DIM = 64
LANES = 128
HEADS_PER_GROUP = LANES // HEAD_DIM
SUBLANES = 8
N_MEM_HEADS = 4
CHUNK = 64
LEFT_CHUNKS = 8
BAND = (LEFT_CHUNKS + 1) * CHUNK
REL_CLIP = 256
CONV_W = 3
LN_X_EPS = 64e-5
RMS_EPS = 1e-6
MASK_VALUE = -1e30

SCAN_CHUNK = 128
SCAN_CHUNKS_PER_STEP = 8
ROW_TILE = 512
PRE_SUBTILES = 2
ATTN_TILE = LEFT_CHUNKS * CHUNK
FFN_COLS = 256
VMEM_LIMIT = 56 * 1024 * 1024

NT = (((1,), (1,)), ((), ()))
TN = (((0,), (0,)), ((), ()))
NN = (((1,), (0,)), ((), ()))


def _bdot(a, b, dims=NN):
    return lax.dot_general(a.astype(BF16), b.astype(BF16), dims,
                           preferred_element_type=F32)


def _split(x):
    hi = x.astype(BF16)
    lo = (x - hi.astype(F32)).astype(BF16)
    return hi, lo


def _ones_dot_right(x, ones_bf16):
    hi, lo = _split(x)
    return (jnp.dot(hi, ones_bf16, preferred_element_type=F32)
            + jnp.dot(lo, ones_bf16, preferred_element_type=F32))


def _ones_dot_left(ones_bf16, x):
    hi, lo = _split(x)
    return (jnp.dot(ones_bf16, hi, preferred_element_type=F32)
            + jnp.dot(ones_bf16, lo, preferred_element_type=F32))


def _rms_scale(x):
    return x * lax.rsqrt(jnp.mean(x * x, axis=-1, keepdims=True) + RMS_EPS)


def _shift_rows(t, prev_rows, n):
    rolled = pltpu.roll(t, n, 0)
    row = lax.broadcasted_iota(jnp.int32, t.shape, 0)
    out = rolled
    for j in range(n):
        out = jnp.where(row == j, prev_rows[j:j + 1, :], out)
    return out


def _const_spec(shape):
    nd = len(shape)
    return pl.BlockSpec(shape, lambda *_: (0,) * nd, pipeline_mode=pl.Buffered(1))


def _layer_spec(stacked_shape, layer):
    nd = len(stacked_shape) - 1
    return pl.BlockSpec((None,) + tuple(stacked_shape[1:]), lambda *_: (layer,) + (0,) * nd,
                        pipeline_mode=pl.Buffered(1))


def _params(*sem):
    return pltpu.CompilerParams(dimension_semantics=sem, vmem_limit_bytes=VMEM_LIMIT)


def _mem_kv_kernel(mem_ref, g_ref, w_ref, out_ref):
    mem_n = _rms_scale(mem_ref[...]) * g_ref[...]
    out_ref[...] = _bdot(mem_n, w_ref[...]).astype(out_ref.dtype)


def _mem_kv(mem, mem_norm, w_mem_kv):
    depth, d, n = w_mem_kv.shape
    n_mem = mem.shape[0]
    return pl.pallas_call(
        _mem_kv_kernel,
        grid=(depth,),
        in_specs=[pl.BlockSpec((n_mem, d), lambda l: (0, 0)),
                  pl.BlockSpec((1, d), lambda l: (0, 0)),
                  pl.BlockSpec((None, d, n), lambda l: (l, 0, 0))],
        out_specs=pl.BlockSpec((None, n_mem, n), lambda l: (l, 0, 0)),
        out_shape=jax.ShapeDtypeStruct((depth, n_mem, n), BF16),
        compiler_params=_params("arbitrary"),
    )(mem, mem_norm.reshape(1, d), w_mem_kv.astype(BF16))


def _rwkv_pre_kernel(has_vres, d_mix, *refs):
    if has_vres:
        (x_ref, ln_ref, win_ref, wlh_ref, wld_ref, w2_ref, a2_ref, g2_ref, mu_ref,
         w0_ref, a0_ref, v2_ref, v0_ref, vf_ref,
         r_ref, lw_ref, k_ref, v_ref, a_ref, g_ref, qm_ref, hprev, pprev) = refs
    else:
        (x_ref, ln_ref, win_ref, wlh_ref, wld_ref, w2_ref, a2_ref, g2_ref, mu_ref,
         w0_ref, a0_ref,
         r_ref, lw_ref, k_ref, v_ref, a_ref, g_ref, qm_ref, hprev, pprev) = refs

    @pl.when(pl.program_id(0) == 0)
    def _():
        hprev[...] = jnp.zeros_like(hprev)
        pprev[...] = jnp.zeros_like(pprev)

    sub = x_ref.shape[0] // PRE_SUBTILES
    tiles = [slice(s * sub, (s + 1) * sub) for s in range(PRE_SUBTILES)]
    h_last = hprev[...]
    first = []
    for rows in tiles:
        h = _rms_scale(x_ref[rows, :]) * ln_ref[...]
        dh = _shift_rows(h, h_last, 1) - h
        h_last = h[sub - 1:sub, :]
        hb = h.astype(BF16)
        p = jnp.dot(hb, win_ref[...], preferred_element_type=F32)
        l1 = (jnp.dot(hb, wlh_ref[...], preferred_element_type=F32)
              + jnp.dot(dh.astype(BF16), wld_ref[...], preferred_element_type=F32))
        first.append((p, l1))
    hprev[...] = h_last

    mu = mu_ref[...]
    p_last = pprev[...]
    for rows, (p, l1) in zip(tiles, first):
        prkv = p[:, :3 * d_mix]
        ps = _shift_rows(prkv, p_last, 1)
        p_last = prkv[sub - 1:sub, :]

        def lerp(j):
            cur = prkv[:, j * d_mix:(j + 1) * d_mix]
            return cur + (ps[:, j * d_mix:(j + 1) * d_mix] - cur) * mu[j:j + 1, :]

        r_ref[rows, :] = lerp(0)
        k_ref[rows, :] = lerp(1)
        v = lerp(2)
        qm_ref[rows, :] = p[:, 3 * d_mix:].astype(qm_ref.dtype)

        l_wa = l1[:, :LANES]
        z = w0_ref[...] + _bdot(jnp.tanh(l_wa), w2_ref[...])
        lw_ref[rows, :] = -math.exp(-0.5) * jax.nn.sigmoid(z)
        a_ref[rows, :] = jax.nn.sigmoid(a0_ref[...] + _bdot(l_wa, a2_ref[...]))
        g_ref[rows, :] = _bdot(jax.nn.sigmoid(l1[:, LANES:2 * LANES]), g2_ref[...])
        if has_vres:
            gate = jax.nn.sigmoid(v0_ref[...]
                                  + _bdot(l1[:, 2 * LANES:3 * LANES], v2_ref[...]))
            v = v + (vf_ref[rows, :] - v) * gate
        v_ref[rows, :] = v
    pprev[...] = p_last


def _pad_rows(w, rows, offset):
    out = jnp.zeros((rows, w.shape[1]), w.dtype)
    return out.at[offset:offset + w.shape[0]].set(w)


def _rwkv_pre(x, ln1, w_in_all, layer, mu_rkv, mu_x, w0, w1, w2, a0, a1, a2, g1, g2, vres,
              v_first):
    seq, d = x.shape
    d_mix = w0.shape[0]
    d_memq = w_in_all.shape[2] - 3 * d_mix
    lora_w, lora_a, lora_g = w1.shape[1], a1.shape[1], g1.shape[1]
    assert lora_w + lora_a == LANES and lora_g == LANES
    has_vres = vres is not None
    firsts = [w1, a1, g1]
    mus = [mu_x[0], mu_x[1], mu_x[2]]
    if has_vres:
        mu_v, v0, v1, v2 = vres
        assert v1.shape[1] <= LANES
        firsts.append(jnp.pad(v1, ((0, 0), (0, LANES - v1.shape[1]))))
        mus.append(mu_v)
    wl_h = jnp.concatenate(firsts, axis=1)
    wl_d = jnp.concatenate([m[:, None] * w for m, w in zip(mus, firsts)], axis=1)
    lp = wl_h.shape[1]
    w2p = _pad_rows(w2, LANES, 0)
    a2p = _pad_rows(a2, LANES, lora_w)

    tm = min(ROW_TILE, seq)
    assert seq % tm == 0
    row = lambda n: pl.BlockSpec((tm, n), lambda i: (i, 0))
    ins = [x, ln1.reshape(1, d), w_in_all, wl_h.astype(BF16), wl_d.astype(BF16),
           w2p.astype(BF16), a2p.astype(BF16), g2.astype(BF16), mu_rkv,
           w0.reshape(1, d_mix), a0.reshape(1, d_mix)]
    specs = [row(d), _const_spec((1, d)), _layer_spec(w_in_all.shape, layer),
             _const_spec((d, lp)),
             _const_spec((d, lp)), _const_spec((LANES, d_mix)), _const_spec((LANES, d_mix)),
             _const_spec((LANES, d_mix)), _const_spec(mu_rkv.shape),
             _const_spec((1, d_mix)), _const_spec((1, d_mix))]
    if has_vres:
        ins += [_pad_rows(v2, LANES, 0).astype(BF16), v0.reshape(1, d_mix), v_first]
        specs += [_const_spec((LANES, d_mix)), _const_spec((1, d_mix)), row(d_mix)]
    mix_out = jax.ShapeDtypeStruct((seq, d_mix), F32)
    return pl.pallas_call(
        functools.partial(_rwkv_pre_kernel, has_vres, d_mix),
        grid=(seq // tm,),
        in_specs=specs,
        out_specs=[row(d_mix)] * 6 + [row(d_memq)],
        out_shape=[mix_out] * 6 + [jax.ShapeDtypeStruct((seq, d_memq), BF16)],
        scratch_shapes=[pltpu.VMEM((1, d), F32), pltpu.VMEM((1, 3 * d_mix), F32)],
        compiler_params=_params("arbitrary"),
    )(*ins)


def _scan_chunks(chunks, kk_w, ka_w, rk_w, lnw, lnb, state, consts):
    tri_incl_b, strict, incl, eye, head_ones, same_head, head0 = consts
    n = SCAN_CHUNK
    heads = range(HEADS_PER_GROUP)
    head_mask = [head0, jnp.logical_not(head0)]
    nc = len(chunks)

    ones2 = jnp.concatenate([head_ones, head_ones], axis=0)

    def head_sums(xs):
        lhs = jnp.concatenate([jnp.concatenate(_split(x), axis=1) for x in xs], axis=0)
        tot = jnp.dot(lhs, ones2, preferred_element_type=F32)
        return [tot[i * n:(i + 1) * n] for i in range(len(xs))]

    kk = [k * kk_w for (_, _, k, _, _) in chunks]
    kmod = [k * (1.0 + (a - 1.0) * ka_w) for (_, _, k, _, a) in chunks]
    sums = head_sums([x * x for x in kk] + [chunks[j][0] * kmod[j] * rk_w for j in range(nc)])
    ss, bonus = sums[:nc], sums[nc:]
    tri2 = jnp.concatenate([tri_incl_b, tri_incl_b], axis=1)
    cum = [jnp.dot(tri2, jnp.concatenate(_split(lw), axis=0), preferred_element_type=F32)
           for (_, lw, _, _, _) in chunks]

    pre = []
    for j, (r, lw, k, v, a) in enumerate(chunks):
        kkn = kk[j] / jnp.maximum(jnp.sqrt(ss[j]), 1e-12)
        c = cum[j]
        ref = c[n // 2 - 1:n // 2, :]
        inv = jnp.exp(ref - c)
        rt = r * jnp.exp(c - ref)
        at = -kkn * jnp.exp(c - lw - ref)
        lhs = [jnp.where(m, x, 0.0).astype(BF16) for m in head_mask for x in (at, rt)]
        pre.append(dict(
            ref=ref, e_last=jnp.exp(c[n - 1:n, :] - ref), rt=rt, at=at.astype(BF16),
            vb=v.astype(BF16), lhs=jnp.concatenate(lhs, axis=0),
            bk=jnp.concatenate([(kkn * a * inv).astype(BF16), (kmod[j] * inv).astype(BF16)],
                               axis=0)))

    quad = [_bdot(pre[j]["lhs"], pre[j]["bk"], NT) for j in range(nc)]
    chains = [(j, h) for j in range(nc) for h in heads]
    strict2 = jnp.concatenate([strict, strict], axis=1)
    incl2 = jnp.concatenate([incl, incl], axis=1)
    ab_ak = {(j, h): jnp.where(strict2, quad[j][2 * h * n:(2 * h + 1) * n], 0.0)
             for j, h in chains}
    rb_rk = {(j, h): jnp.where(incl2, quad[j][(2 * h + 1) * n:(2 * h + 2) * n], 0.0).astype(BF16)
             for j, h in chains}

    t_inv = {ch: eye + ab_ak[ch][:, :n] for ch in chains}
    nb = {ch: ab_ak[ch][:, :n].astype(BF16) for ch in chains}
    power = {ch: _bdot(nb[ch], nb[ch]).astype(BF16) for ch in chains}
    for _ in range(int(math.log2(n)) - 2):
        both = {ch: _bdot(jnp.concatenate([t_inv[ch].astype(BF16), power[ch]], axis=0), power[ch])
                for ch in chains}
        t_inv = {ch: t_inv[ch] + both[ch][:n] for ch in chains}
        power = {ch: both[ch][n:].astype(BF16) for ch in chains}
    t_b = {ch: (t_inv[ch] + _bdot(t_inv[ch], power[ch])).astype(BF16) for ch in chains}

    def by_head(x):
        wide = (lax.broadcasted_iota(jnp.int32, x.shape, 1) % LANES) < HEAD_DIM
        return jnp.concatenate([jnp.where(wide, x, jnp.zeros_like(x)),
                                jnp.where(wide, jnp.zeros_like(x), x)], axis=0)

    def heads_on_k(mats, j):
        return jnp.concatenate([mats[(j, h)] for h in heads], axis=1)

    ak_b = {ch: ab_ak[ch][:, n:].astype(BF16) for ch in chains}
    akv = [_bdot(heads_on_k(ak_b, j), by_head(pre[j]["vb"])).astype(BF16) for j in range(nc)]
    t_x = [_bdot(heads_on_k(t_b, j), by_head(jnp.concatenate([pre[j]["at"], akv[j]], axis=1)))
           for j in range(nc)]
    ahat = [t[:, :LANES].astype(BF16) for t in t_x]
    uin = [t[:, LANES:].astype(BF16) for t in t_x]
    zero = jnp.zeros((n, LANES), BF16)
    r_x = [_bdot(heads_on_k(rb_rk, j),
                 by_head(jnp.concatenate([jnp.concatenate([ahat[j], uin[j]], axis=1),
                                          jnp.concatenate([zero, pre[j]["vb"]], axis=1)], axis=0)))
           for j in range(nc)]
    rhat = [pre[j]["rt"] + r_x[j][:, :LANES] for j in range(nc)]
    yin = [r_x[j][:, LANES:] for j in range(nc)]
    m_mat = [(eye + jnp.where(same_head, _bdot(ahat[j], pre[j]["bk"][:n], TN), 0.0))
             * pre[j]["e_last"] for j in range(nc)]
    c_mat = [jnp.where(same_head,
                       _bdot(jnp.concatenate([uin[j], pre[j]["vb"]], axis=0), pre[j]["bk"], TN),
                       0.0) * pre[j]["e_last"] for j in range(nc)]

    ys = []
    for j in range(nc):
        sp = state * jnp.exp(pre[j]["ref"])
        ys.append(_bdot(rhat[j], sp, NT) + yin[j])
        state = _bdot(sp, m_mat[j]) + c_mat[j]

    inv_hd = 1.0 / HEAD_DIM
    mean = [m * inv_hd for m in head_sums(ys)]
    dy = [y - m for y, m in zip(ys, mean)]
    var = [s * inv_hd for s in head_sums([d * d for d in dy])]
    outs = [dy[j] * lax.rsqrt(var[j] + LN_X_EPS) * lnw + lnb + bonus[j] * chunks[j][3]
            for j in range(nc)]
    return outs, state


def _rwkv_scan_kernel(r_ref, lw_ref, k_ref, v_ref, a_ref, kk_ref, ka_ref, rk_ref,
                      lnw_ref, lnb_ref, y_ref, state_ref):
    @pl.when(pl.program_id(1) == 0)
    def _():
        state_ref[...] = jnp.zeros_like(state_ref)

    n = SCAN_CHUNK
    row = lax.broadcasted_iota(jnp.int32, (n, n), 0)
    col = lax.broadcasted_iota(jnp.int32, (n, n), 1)
    incl = row >= col
    strict = row > col
    same_head = (row // HEAD_DIM) == (col // HEAD_DIM)
    consts = (incl.astype(BF16), strict, incl, (row == col).astype(F32),
              same_head.astype(BF16), same_head, col < HEAD_DIM)

    nc = r_ref.shape[0] // n
    chunks = [tuple(ref[j * n:(j + 1) * n, :] for ref in (r_ref, lw_ref, k_ref, v_ref, a_ref))
              for j in range(nc)]
    outs, state = _scan_chunks(chunks, kk_ref[...], ka_ref[...], rk_ref[...],
                               lnw_ref[...], lnb_ref[...], state_ref[...], consts)
    for j in range(nc):
        y_ref[j * n:(j + 1) * n, :] = outs[j]
    state_ref[...] = state


def _rwkv_scan(r, lw, k, v, a, k_k, k_a, r_k, lnx_w, lnx_b):
    seq, d_mix = r.shape
    assert LANES == SCAN_CHUNK and d_mix % LANES == 0
    rows = min(SCAN_CHUNK * SCAN_CHUNKS_PER_STEP, seq)
    assert seq % rows == 0 and rows % SCAN_CHUNK == 0
    act = pl.BlockSpec((rows, LANES), lambda p, c: (c, p))
    par = pl.BlockSpec((1, LANES), lambda p, c: (0, p))
    flat = lambda t: t.reshape(1, d_mix)
    return pl.pallas_call(
        _rwkv_scan_kernel,
        grid=(d_mix // LANES, seq // rows),
        in_specs=[act] * 5 + [par] * 5,
        out_specs=act,
        out_shape=jax.ShapeDtypeStruct((seq, d_mix), F32),
        scratch_shapes=[pltpu.VMEM((LANES, LANES), F32)],
        compiler_params=_params("parallel", "arbitrary"),
    )(r, lw, k, v, a, flat(k_k), flat(k_a), flat(r_k), flat(lnx_w), flat(lnx_b))


def _norm_proj_kernel(n_proj, *refs):
    x_ref = refs[0]
    gains = refs[1:1 + n_proj]
    weights = refs[1 + n_proj:1 + 2 * n_proj]
    outs = refs[1 + 2 * n_proj:]
    xn = _rms_scale(x_ref[...])
    o = 0
    for g_ref, w_ref in zip(gains, weights):
        y = _bdot(xn * g_ref[...], w_ref[...])
        start = 0
        while start < y.shape[1]:
            width = outs[o].shape[1]
            outs[o][...] = y[:, start:start + width].astype(outs[o].dtype)
            start += width
            o += 1


def _norm_proj(x, projections):
    seq, d = x.shape
    tm = min(ROW_TILE, seq)
    row = lambda n: pl.BlockSpec((tm, n), lambda i: (i, 0))
    gains = [g.reshape(1, d) for g, _, _ in projections]
    weights = [w.astype(BF16) for _, w, _ in projections]
    widths = [n for _, _, ws in projections for n in ws]
    return pl.pallas_call(
        functools.partial(_norm_proj_kernel, len(projections)),
        grid=(seq // tm,),
        in_specs=([row(d)] + [_const_spec((1, d))] * len(gains)
                  + [_const_spec(w.shape) for w in weights]),
        out_specs=[row(n) for n in widths],
        out_shape=[jax.ShapeDtypeStruct((seq, n), BF16) for n in widths],
        compiler_params=_params("arbitrary"),
    )(x, *gains, *weights)


def _band_bias(table):
    ext = jnp.concatenate(
        [table, jnp.broadcast_to(table[:, -1:], (table.shape[0], BAND + CHUNK - 1 - table.shape[1]))],
        axis=1).astype(F32)
    rows = [ext[:, i:i + BAND] for i in range(CHUNK)]
    return jnp.stack(rows, axis=1)[:, :, ::-1]


def _chunk_attn_tile(first_tile, q_ref, kp_ref, kc_ref, vp_ref, vc_ref, bias_ref, o_ref):
    tq = q_ref.shape[0]
    k_win = jnp.concatenate([kp_ref[...], kc_ref[...]], axis=0)
    v_win = jnp.concatenate([vp_ref[...], vc_ref[...]], axis=0)
    bias = jnp.concatenate([bias_ref[hd] for hd in range(HEADS_PER_GROUP)], axis=0)
    lane = lax.broadcasted_iota(jnp.int32, (CHUNK, LANES), 1)
    col = lax.broadcasted_iota(jnp.int32, (HEADS_PER_GROUP * CHUNK, BAND), 1)
    scale = HEAD_DIM ** -0.5
    blocks = range(tq // CHUNK)
    scores = []
    for c in blocks:
        q = q_ref[c * CHUNK:(c + 1) * CHUNK, :] * scale
        q2 = jnp.concatenate(
            [jnp.where((lane // HEAD_DIM) == hd, q, jnp.zeros_like(q))
             for hd in range(HEADS_PER_GROUP)], axis=0)
        scores.append(_bdot(q2, k_win[c * CHUNK:c * CHUNK + BAND, :], NT))
    probs, denoms = [], []
    for c in blocks:
        s = scores[c] + bias
        if first_tile:
            s = jnp.where(col >= tq - c * CHUNK, s, MASK_VALUE)
        e = jnp.exp(s - jnp.max(s, axis=-1, keepdims=True))
        denoms.append(jnp.sum(e, axis=-1, keepdims=True))
        probs.append(e.astype(BF16))
    for c in blocks:
        o2 = _bdot(probs[c], v_win[c * CHUNK:c * CHUNK + BAND, :]) / denoms[c]
        out = o2[:CHUNK, :]
        for hd in range(1, HEADS_PER_GROUP):
            out = jnp.where((lane // HEAD_DIM) == hd, o2[hd * CHUNK:(hd + 1) * CHUNK, :], out)
        o_ref[c * CHUNK:(c + 1) * CHUNK, :] = out


def _chunk_attn_kernel(*refs):
    is_first = pl.program_id(1) == 0
    pl.when(is_first)(functools.partial(_chunk_attn_tile, True, *refs))
    pl.when(jnp.logical_not(is_first))(functools.partial(_chunk_attn_tile, False, *refs))


def _chunk_attn(q, k, v, rel_table):
    seq, d_mix = q.shape
    tq = ATTN_TILE
    assert seq % tq == 0
    cur = pl.BlockSpec((tq, LANES), lambda p, i: (i, p))
    prev = pl.BlockSpec((tq, LANES), lambda p, i: (jnp.maximum(i - 1, 0), p))
    bias = _band_bias(rel_table)
    return pl.pallas_call(
        _chunk_attn_kernel,
        grid=(d_mix // LANES, seq // tq),
        in_specs=[cur, prev, cur, prev, cur,
                  pl.BlockSpec((HEADS_PER_GROUP, CHUNK, BAND), lambda p, i: (p, 0, 0))],
        out_specs=cur,
        out_shape=jax.ShapeDtypeStruct((seq, d_mix), F32),
        compiler_params=_params("parallel", "arbitrary"),
    )(q, k, k, v, v, bias)


def _layer_post_kernel(has_gate, has_final, d_mix, d_ff, *refs):
    refs = list(refs)
    x_ref, mix_ref = refs[:2]
    refs = refs[2:]
    gate_ref = refs.pop(0) if has_gate else None
    (qm_ref, kvm_ref, wout_ref, ln2_ref, fin_ref, cw_ref, cb_ref, fout_ref) = refs[:8]
    refs = refs[8:]
    lnf_ref = refs.pop(0) if has_final else None
    out_ref, carry = refs

    @pl.when(pl.program_id(0) == 0)
    def _():
        carry[...] = jnp.zeros_like(carry)

    tm = x_ref.shape[0]
    d_memq = qm_ref.shape[1]
    mix = mix_ref[...]
    if has_gate:
        mix = mix * gate_ref[...]

    proj_mix = _bdot(mix, wout_ref[:d_mix, :])

    qm = qm_ref[...] * HEAD_DIM ** -0.5
    km = kvm_ref[:, :d_memq]
    vm = kvm_ref[:, d_memq:]
    lane = lax.broadcasted_iota(jnp.int32, qm.shape, 1)
    head_masks = [(lane // HEAD_DIM) == hd for hd in range(d_memq // HEAD_DIM)]
    scores = [_bdot(jnp.where(mh, qm, jnp.zeros_like(qm)), km, NT) for mh in head_masks]
    exps = [jnp.exp(s - jnp.max(s, axis=-1, keepdims=True)) for s in scores]
    heads_out = [_bdot(e, vm) / jnp.sum(e, axis=-1, keepdims=True) for e in exps]
    mo = heads_out[0]
    for mh, o_h in zip(head_masks[1:], heads_out[1:]):
        mo = jnp.where(mh, o_h, mo)
    x1 = x_ref[...] + (proj_mix + _bdot(mo, wout_ref[d_mix:, :]))

    h2 = (_rms_scale(x1) * ln2_ref[...]).astype(BF16)

    def in_dots(c0):
        return (jnp.dot(h2, fin_ref[:, c0:c0 + FFN_COLS], preferred_element_type=F32),
                jnp.dot(h2, fin_ref[:, d_ff + c0:d_ff + c0 + FFN_COLS],
                        preferred_element_type=F32))

    slabs = list(range(0, d_ff, FFN_COLS))
    ffn = None
    nxt = in_dots(slabs[0])
    for idx, c0 in enumerate(slabs):
        cols = slice(c0, c0 + FFN_COLS)
        gate, val = nxt
        if idx + 1 < len(slabs):
            nxt = in_dots(slabs[idx + 1])
        prev = carry[SUBLANES - (CONV_W - 1):, cols]
        carry[:, cols] = gate[tm - SUBLANES:, :]
        conv = cb_ref[:, cols] + cw_ref[CONV_W - 1:CONV_W, cols] * gate
        for j in range(1, CONV_W):
            conv = conv + (cw_ref[CONV_W - 1 - j:CONV_W - j, cols]
                           * _shift_rows(gate, prev[CONV_W - 1 - j:, :], j))
        act = 0.5 * conv * (1.0 + lax.erf(conv * (2.0 ** -0.5))) * val
        slab = _bdot(act, fout_ref[cols, :])
        ffn = slab if ffn is None else ffn + slab
    acc = x1 + ffn
    if has_final:
        acc = _rms_scale(acc) * lnf_ref[...]
    out_ref[...] = acc


def _layer_post(x, mix, gate, qm, layer, kv_mem, w_out, ln2, ffn_in, conv_w, conv_b, ffn_out,
                ln_f):
    seq, d = x.shape
    d_mix = mix.shape[1]
    d_memq = qm.shape[1]
    d_ff = ffn_out.shape[1]
    assert d_ff % FFN_COLS == 0 and conv_w.shape[0] == CONV_W
    tm = min(ROW_TILE, seq)
    row = lambda n: pl.BlockSpec((tm, n), lambda i: (i, 0))
    ins, specs = [x, mix], [row(d), row(d_mix)]
    if gate is not None:
        ins.append(gate)
        specs.append(row(d_mix))
    ins += [qm, kv_mem, w_out, ln2.reshape(1, d), ffn_in, conv_w, conv_b.reshape(1, d_ff),
            ffn_out]
    specs += [row(d_memq), _layer_spec(kv_mem.shape, layer), _layer_spec(w_out.shape, layer),
              _const_spec((1, d)), _layer_spec(ffn_in.shape, layer), _const_spec(conv_w.shape),
              _const_spec((1, d_ff)), _layer_spec(ffn_out.shape, layer)]
    if ln_f is not None:
        ins.append(ln_f.reshape(1, d))
        specs.append(_const_spec((1, d)))
    return pl.pallas_call(
        functools.partial(_layer_post_kernel, gate is not None, ln_f is not None, d_mix, d_ff),
        grid=(seq // tm,),
        in_specs=specs,
        out_specs=row(d),
        out_shape=jax.ShapeDtypeStruct((seq, d), F32),
        scratch_shapes=[pltpu.VMEM((SUBLANES, d_ff), F32)],
        compiler_params=_params("arbitrary"),
    )(*ins)


def kernel(x, mem, mem_norm, ln1, ln2, w_out, w_mem_kv, ffn_in, ffn_conv, ffn_conv_b, ffn_out, a_w_in, a_mu_rkv, a_mu_x, a_w0, a_w1, a_w2, a_a0, a_a1, a_a2, a_g1, a_g2, a_k_k, a_k_a, a_r_k, a_lnx_w, a_lnx_b, a_mu_v, a_v0, a_v1, a_v2, ln_kv, w_kv, b_w_in, b_rel, ln_f):
    bsz, seq, d = x.shape
    assert bsz == 1 and mem.shape[0] == 1
    depth = ln1.shape[0]
    n_a = a_w_in.shape[0]
    d_mix = a_w0.shape[1]
    xs = x.reshape(seq, d)
    kv_mem = _mem_kv(mem.reshape(mem.shape[1], d), mem_norm, w_mem_kv)
    a_w_in_b, w_out_b = a_w_in.astype(BF16), w_out.astype(BF16)
    ffn_in_b, ffn_out_b = ffn_in.astype(BF16), ffn_out.astype(BF16)

    v_first = None
    k_s = v_s = None
    for layer in range(depth):
        last = ln_f if layer == depth - 1 else None
        if layer < n_a:
            i = layer
            vres = None if i == 0 else (a_mu_v[i - 1], a_v0[i - 1], a_v1[i - 1], a_v2[i - 1])
            r, lw, k, v, a, g, qm = _rwkv_pre(
                xs, ln1[layer], a_w_in_b, i, a_mu_rkv[i], a_mu_x[i], a_w0[i], a_w1[i], a_w2[i],
                a_a0[i], a_a1[i], a_a2[i], a_g1[i], a_g2[i], vres, v_first)
            if i == 0:
                v_first = v
            mix = _rwkv_scan(r, lw, k, v, a, a_k_k[i], a_k_a[i], a_r_k[i],
                             a_lnx_w[i], a_lnx_b[i])
            gate = g
        else:
            j = layer - n_a
            d_memq = b_w_in.shape[2] - d_mix
            projections = [(ln1[layer], b_w_in[j], [d_mix, d_memq])]
            if j == 0:
                projections.append((ln_kv, w_kv, [d_mix, d_mix]))
            outs = _norm_proj(xs, projections)
            q, qm = outs[:2]
            if j == 0:
                k_s, v_s = outs[2:]
            mix = _chunk_attn(q, k_s, v_s, b_rel[j])
            gate = None
        xs = _layer_post(xs, mix, gate, qm, layer, kv_mem, w_out_b, ln2[layer], ffn_in_b,
                         ffn_conv[layer], ffn_conv_b[layer], ffn_out_b, last)
    return xs.reshape(bsz, seq, d)
```

```python
import functools
import math

import jax
import jax.numpy as jnp
from jax import lax
from jax.experimental import pallas as pl
from jax.experimental.pallas import tpu as pltpu

F32 = jnp.float32
BF16 = jnp.bfloat16

HEAD_DIM = 64
LANES = 128
HEADS_PER_GROUP = LANES // HEAD_DIM
SUBLANES = 8
N_MEM_HEADS = 4
CHUNK = 64
LEFT_CHUNKS = 8
BAND = (LEFT_CHUNKS + 1) * CHUNK
REL_CLIP = 256
CONV_W = 3
LN_X_EPS = 64e-5
RMS_EPS = 1e-6
MASK_VALUE = -1e30

SCAN_CHUNK = 128
SCAN_CHUNKS_PER_STEP = 8
ROW_TILE = 512
POST_TILE = 1024
PRE_SUBTILES = 2
ATTN_TILE = LEFT_CHUNKS * CHUNK
FFN_COLS = 256
VMEM_LIMIT = 56 * 1024 * 1024

NT = (((1,), (1,)), ((), ()))
TN = (((0,), (0,)), ((), ()))
NN = (((1,), (0,)), ((), ()))


def _bdot(a, b, dims=NN):
    return lax.dot_general(a.astype(BF16), b.astype(BF16), dims,
                           preferred_element_type=F32)


def _split(x):
    hi = x.astype(BF16)
    lo = (x - hi.astype(F32)).astype(BF16)
    return hi, lo


def _ones_dot_right(x, ones_bf16):
    hi, lo = _split(x)
    return (jnp.dot(hi, ones_bf16, preferred_element_type=F32)
            + jnp.dot(lo, ones_bf16, preferred_element_type=F32))


def _ones_dot_left(ones_bf16, x):
    hi, lo = _split(x)
    return (jnp.dot(ones_bf16, hi, preferred_element_type=F32)
            + jnp.dot(ones_bf16, lo, preferred_element_type=F32))


def _rms_scale(x):
    return x * lax.rsqrt(jnp.mean(x * x, axis=-1, keepdims=True) + RMS_EPS)


def _shift_rows(t, prev_rows, n):
    rolled = pltpu.roll(t, n, 0)
    row = lax.broadcasted_iota(jnp.int32, t.shape, 0)
    out = rolled
    for j in range(n):
        out = jnp.where(row == j, prev_rows[j:j + 1, :], out)
    return out


def _const_spec(shape):
    nd = len(shape)
    return pl.BlockSpec(shape, lambda *_: (0,) * nd, pipeline_mode=pl.Buffered(1))


def _layer_spec(stacked_shape, layer):
    nd = len(stacked_shape) - 1
    return pl.BlockSpec((None,) + tuple(stacked_shape[1:]), lambda *_: (layer,) + (0,) * nd,
                        pipeline_mode=pl.Buffered(1))


def _params(*sem):
    return pltpu.CompilerParams(dimension_semantics=sem, vmem_limit_bytes=VMEM_LIMIT)


def _mem_kv_kernel(mem_ref, g_ref, w_ref, out_ref):
    mem_n = _rms_scale(mem_ref[...]) * g_ref[...]
    out_ref[...] = _bdot(mem_n, w_ref[...]).astype(out_ref.dtype)


def _mem_kv(mem, mem_norm, w_mem_kv):
    depth, d, n = w_mem_kv.shape
    n_mem = mem.shape[0]
    return pl.pallas_call(
        _mem_kv_kernel,
        grid=(depth,),
        in_specs=[pl.BlockSpec((n_mem, d), lambda l: (0, 0)),
                  pl.BlockSpec((1, d), lambda l: (0, 0)),
                  pl.BlockSpec((None, d, n), lambda l: (l, 0, 0))],
        out_specs=pl.BlockSpec((None, n_mem, n), lambda l: (l, 0, 0)),
        out_shape=jax.ShapeDtypeStruct((depth, n_mem, n), BF16),
        compiler_params=_params("arbitrary"),
    )(mem, mem_norm.reshape(1, d), w_mem_kv.astype(BF16))


def _rwkv_pre_kernel(has_vres, d_mix, *refs):
    if has_vres:
        (x_ref, ln_ref, win_ref, wlh_ref, wld_ref, w2_ref, a2_ref, g2_ref, mu_ref,
         w0_ref, a0_ref, v2_ref, v0_ref, vf_ref,
         r_ref, lw_ref, k_ref, v_ref, a_ref, g_ref, qm_ref, hprev, pprev) = refs
    else:
        (x_ref, ln_ref, win_ref, wlh_ref, wld_ref, w2_ref, a2_ref, g2_ref, mu_ref,
         w0_ref, a0_ref,
         r_ref, lw_ref, k_ref, v_ref, a_ref, g_ref, qm_ref, hprev, pprev) = refs

    @pl.when(pl.program_id(0) == 0)
    def _():
        hprev[...] = jnp.zeros_like(hprev)
        pprev[...] = jnp.zeros_like(pprev)

    sub = x_ref.shape[0] // PRE_SUBTILES
    tiles = [slice(s * sub, (s + 1) * sub) for s in range(PRE_SUBTILES)]
    h_last = hprev[...]
    first = []
    for rows in tiles:
        h = _rms_scale(x_ref[rows, :]) * ln_ref[...]
        dh = _shift_rows(h, h_last, 1) - h
        h_last = h[sub - 1:sub, :]
        hb = h.astype(BF16)
        p = jnp.dot(hb, win_ref[...], preferred_element_type=F32)
        l1 = (jnp.dot(hb, wlh_ref[...], preferred_element_type=F32)
              + jnp.dot(dh.astype(BF16), wld_ref[...], preferred_element_type=F32))
        first.append((p, l1))
    hprev[...] = h_last

    mu = mu_ref[...]
    p_last = pprev[...]
    for rows, (p, l1) in zip(tiles, first):
        prkv = p[:, :3 * d_mix]
        ps = _shift_rows(prkv, p_last, 1)
        p_last = prkv[sub - 1:sub, :]

        def lerp(j):
            cur = prkv[:, j * d_mix:(j + 1) * d_mix]
            return cur + (ps[:, j * d_mix:(j + 1) * d_mix] - cur) * mu[j:j + 1, :]

        r_ref[rows, :] = lerp(0)
        k_ref[rows, :] = lerp(1)
        v = lerp(2)
        qm_ref[rows, :] = p[:, 3 * d_mix:].astype(qm_ref.dtype)

        l_wa = l1[:, :LANES]
        z = w0_ref[...] + _bdot(jnp.tanh(l_wa), w2_ref[...])
        lw_ref[rows, :] = -math.exp(-0.5) * jax.nn.sigmoid(z)
        a_ref[rows, :] = jax.nn.sigmoid(a0_ref[...] + _bdot(l_wa, a2_ref[...]))
        g_ref[rows, :] = _bdot(jax.nn.sigmoid(l1[:, LANES:2 * LANES]), g2_ref[...])
        if has_vres:
            gate = jax.nn.sigmoid(v0_ref[...]
                                  + _bdot(l1[:, 2 * LANES:3 * LANES], v2_ref[...]))
            v = v + (vf_ref[rows, :] - v) * gate
        v_ref[rows, :] = v
    pprev[...] = p_last


def _pad_rows(w, rows, offset):
    out = jnp.zeros((rows, w.shape[1]), w.dtype)
    return out.at[offset:offset + w.shape[0]].set(w)


def _rwkv_pre(x, ln1, w_in_all, layer, mu_rkv, mu_x, w0, w1, w2, a0, a1, a2, g1, g2, vres,
              v_first):
    seq, d = x.shape
    d_mix = w0.shape[0]
    d_memq = w_in_all.shape[2] - 3 * d_mix
    lora_w, lora_a, lora_g = w1.shape[1], a1.shape[1], g1.shape[1]
    assert lora_w + lora_a == LANES and lora_g == LANES
    has_vres = vres is not None
    firsts = [w1, a1, g1]
    mus = [mu_x[0], mu_x[1], mu_x[2]]
    if has_vres:
        mu_v, v0, v1, v2 = vres
        assert v1.shape[1] <= LANES
        firsts.append(jnp.pad(v1, ((0, 0), (0, LANES - v1.shape[1]))))
        mus.append(mu_v)
    wl_h = jnp.concatenate(firsts, axis=1)
    wl_d = jnp.concatenate([m[:, None] * w for m, w in zip(mus, firsts)], axis=1)
    lp = wl_h.shape[1]
    w2p = _pad_rows(w2, LANES, 0)
    a2p = _pad_rows(a2, LANES, lora_w)

    tm = min(ROW_TILE, seq)
    assert seq % tm == 0
    row = lambda n: pl.BlockSpec((tm, n), lambda i: (i, 0))
    ins = [x, ln1.reshape(1, d), w_in_all, wl_h.astype(BF16), wl_d.astype(BF16),
           w2p.astype(BF16), a2p.astype(BF16), g2.astype(BF16), mu_rkv,
           w0.reshape(1, d_mix), a0.reshape(1, d_mix)]
    specs = [row(d), _const_spec((1, d)), _layer_spec(w_in_all.shape, layer),
             _const_spec((d, lp)),
             _const_spec((d, lp)), _const_spec((LANES, d_mix)), _const_spec((LANES, d_mix)),
             _const_spec((LANES, d_mix)), _const_spec(mu_rkv.shape),
             _const_spec((1, d_mix)), _const_spec((1, d_mix))]
    if has_vres:
        ins += [_pad_rows(v2, LANES, 0).astype(BF16), v0.reshape(1, d_mix), v_first]
        specs += [_const_spec((LANES, d_mix)), _const_spec((1, d_mix)), row(d_mix)]
    mix_out = jax.ShapeDtypeStruct((seq, d_mix), F32)
    return pl.pallas_call(
        functools.partial(_rwkv_pre_kernel, has_vres, d_mix),
        grid=(seq // tm,),
        in_specs=specs,
        out_specs=[row(d_mix)] * 6 + [row(d_memq)],
        out_shape=[mix_out] * 6 + [jax.ShapeDtypeStruct((seq, d_memq), BF16)],
        scratch_shapes=[pltpu.VMEM((1, d), F32), pltpu.VMEM((1, 3 * d_mix), F32)],
        compiler_params=_params("arbitrary"),
    )(*ins)


def _scan_chunks(chunks, gates, kk_w, ka_w, rk_w, lnw, lnb, state, consts):
    tri_incl_b, strict, incl, eye, head_ones, same_head, head0 = consts
    n = SCAN_CHUNK
    heads = range(HEADS_PER_GROUP)
    head_mask = [head0, jnp.logical_not(head0)]
    nc = len(chunks)

    ones2 = jnp.concatenate([head_ones, head_ones], axis=0)

    def head_sums(xs):
        lhs = jnp.concatenate([jnp.concatenate(_split(x), axis=1) for x in xs], axis=0)
        tot = jnp.dot(lhs, ones2, preferred_element_type=F32)
        return [tot[i * n:(i + 1) * n] for i in range(len(xs))]

    kk = [k * kk_w for (_, _, k, _, _) in chunks]
    kmod = [k * (1.0 + (a - 1.0) * ka_w) for (_, _, k, _, a) in chunks]
    sums = head_sums([x * x for x in kk] + [chunks[j][0] * kmod[j] * rk_w for j in range(nc)])
    ss, bonus = sums[:nc], sums[nc:]
    tri2 = jnp.concatenate([tri_incl_b, tri_incl_b], axis=1)
    cum = [jnp.dot(tri2, jnp.concatenate(_split(lw), axis=0), preferred_element_type=F32)
           for (_, lw, _, _, _) in chunks]

    pre = []
    for j, (r, lw, k, v, a) in enumerate(chunks):
        kkn = kk[j] / jnp.maximum(jnp.sqrt(ss[j]), 1e-12)
        c = cum[j]
        ref = c[n // 2 - 1:n // 2, :]
        inv = jnp.exp(ref - c)
        rt = r * jnp.exp(c - ref)
        at = -kkn * jnp.exp(c - lw - ref)
        lhs = [jnp.where(m, x, 0.0).astype(BF16) for m in head_mask for x in (at, rt)]
        pre.append(dict(
            ref=ref, e_last=jnp.exp(c[n - 1:n, :] - ref), rt=rt, at=at.astype(BF16),
            vb=v.astype(BF16), lhs=jnp.concatenate(lhs, axis=0),
            bk=jnp.concatenate([(kkn * a * inv).astype(BF16), (kmod[j] * inv).astype(BF16)],
                               axis=0)))

    quad = [_bdot(pre[j]["lhs"], pre[j]["bk"], NT) for j in range(nc)]
    chains = [(j, h) for j in range(nc) for h in heads]
    strict2 = jnp.concatenate([strict, strict], axis=1)
    incl2 = jnp.concatenate([incl, incl], axis=1)
    ab_ak = {(j, h): jnp.where(strict2, quad[j][2 * h * n:(2 * h + 1) * n], 0.0)
             for j, h in chains}
    rb_rk = {(j, h): jnp.where(incl2, quad[j][(2 * h + 1) * n:(2 * h + 2) * n], 0.0).astype(BF16)
             for j, h in chains}

    half = n // 2
    left = lax.broadcasted_iota(jnp.int32, (half, n), 1) < half

    def block_diag(packed):
        return jnp.concatenate([jnp.where(left, packed, jnp.zeros_like(packed)),
                                jnp.where(left, jnp.zeros_like(packed), packed)], axis=0)

    eye_packed = jnp.where(left, eye[:half], eye[half:])
    nmat = {ch: ab_ak[ch][:, :n] for ch in chains}
    packed = {ch: jnp.where(left, nmat[ch][:half], nmat[ch][half:]) for ch in chains}
    t_p = {ch: eye_packed + packed[ch] for ch in chains}
    pb = {ch: packed[ch].astype(BF16) for ch in chains}
    power = {ch: _bdot(pb[ch], block_diag(pb[ch])).astype(BF16) for ch in chains}
    for _ in range(int(math.log2(half)) - 2):
        both = {ch: _bdot(jnp.concatenate([t_p[ch].astype(BF16), power[ch]], axis=0),
                          block_diag(power[ch])) for ch in chains}
        t_p = {ch: t_p[ch] + both[ch][:half] for ch in chains}
        power = {ch: both[ch][half:].astype(BF16) for ch in chains}
    t_p = {ch: t_p[ch] + _bdot(t_p[ch], block_diag(power[ch])) for ch in chains}
    tp_b = {ch: t_p[ch].astype(BF16) for ch in chains}
    n21_t1 = {ch: _bdot(jnp.where(left, nmat[ch][half:], 0.0), block_diag(tp_b[ch]))
              for ch in chains}
    zeros_half = jnp.zeros((half, n), BF16)
    t21 = {ch: _bdot(jnp.where(left, jnp.zeros_like(tp_b[ch]), tp_b[ch]),
                     jnp.concatenate([zeros_half, n21_t1[ch].astype(BF16)], axis=0))
           for ch in chains}
    t_b = {ch: jnp.concatenate(
        [jnp.where(left, t_p[ch], 0.0),
         t21[ch] + jnp.where(left, 0.0, t_p[ch])], axis=0).astype(BF16) for ch in chains}

    def by_head(x):
        wide = (lax.broadcasted_iota(jnp.int32, x.shape, 1) % LANES) < HEAD_DIM
        return jnp.concatenate([jnp.where(wide, x, jnp.zeros_like(x)),
                                jnp.where(wide, jnp.zeros_like(x), x)], axis=0)

    def heads_on_k(mats, j):
        return jnp.concatenate([mats[(j, h)] for h in heads], axis=1)

    ak_b = {ch: ab_ak[ch][:, n:].astype(BF16) for ch in chains}
    akv = [_bdot(heads_on_k(ak_b, j), by_head(pre[j]["vb"])).astype(BF16) for j in range(nc)]
    t_x = [_bdot(heads_on_k(t_b, j), by_head(jnp.concatenate([pre[j]["at"], akv[j]], axis=1)))
           for j in range(nc)]
    ahat = [t[:, :LANES].astype(BF16) for t in t_x]
    uin = [t[:, LANES:].astype(BF16) for t in t_x]
    zero = jnp.zeros((n, LANES), BF16)
    r_x = [_bdot(heads_on_k(rb_rk, j),
                 by_head(jnp.concatenate([jnp.concatenate([ahat[j], uin[j]], axis=1),
                                          jnp.concatenate([zero, pre[j]["vb"]], axis=1)], axis=0)))
           for j in range(nc)]
    rhat = [pre[j]["rt"] + r_x[j][:, :LANES] for j in range(nc)]
    yin = [r_x[j][:, LANES:] for j in range(nc)]
    m_mat = [(eye + jnp.where(same_head, _bdot(ahat[j], pre[j]["bk"][:n], TN), 0.0))
             * pre[j]["e_last"] for j in range(nc)]
    c_mat = [jnp.where(same_head,
                       _bdot(jnp.concatenate([uin[j], pre[j]["vb"]], axis=0), pre[j]["bk"], TN),
                       0.0) * pre[j]["e_last"] for j in range(nc)]

    ys = []
    for j in range(nc):
        sp = state * jnp.exp(pre[j]["ref"])
        ys.append(_bdot(rhat[j], sp, NT) + yin[j])
        state = _bdot(sp, m_mat[j]) + c_mat[j]

    inv_hd = 1.0 / HEAD_DIM
    mean = [m * inv_hd for m in head_sums(ys)]
    dy = [y - m for y, m in zip(ys, mean)]
    var = [s * inv_hd for s in head_sums([d * d for d in dy])]
    outs = [(dy[j] * lax.rsqrt(var[j] + LN_X_EPS) * lnw + lnb + bonus[j] * chunks[j][3])
            * gates[j] for j in range(nc)]
    return outs, state


def _rwkv_scan_kernel(r_ref, lw_ref, k_ref, v_ref, a_ref, g_ref, kk_ref, ka_ref, rk_ref,
                      lnw_ref, lnb_ref, y_ref, state_ref):
    @pl.when(pl.program_id(1) == 0)
    def _():
        state_ref[...] = jnp.zeros_like(state_ref)

    n = SCAN_CHUNK
    row = lax.broadcasted_iota(jnp.int32, (n, n), 0)
    col = lax.broadcasted_iota(jnp.int32, (n, n), 1)
    incl = row >= col
    strict = row > col
    same_head = (row // HEAD_DIM) == (col // HEAD_DIM)
    consts = (incl.astype(BF16), strict, incl, (row == col).astype(F32),
              same_head.astype(BF16), same_head, col < HEAD_DIM)

    nc = r_ref.shape[0] // n
    chunks = [tuple(ref[j * n:(j + 1) * n, :] for ref in (r_ref, lw_ref, k_ref, v_ref, a_ref))
              for j in range(nc)]
    gates = [g_ref[j * n:(j + 1) * n, :] for j in range(nc)]
    outs, state = _scan_chunks(chunks, gates, kk_ref[...], ka_ref[...], rk_ref[...],
                               lnw_ref[...], lnb_ref[...], state_ref[...], consts)
    for j in range(nc):
        y_ref[j * n:(j + 1) * n, :] = outs[j].astype(y_ref.dtype)
    state_ref[...] = state


def _rwkv_scan(r, lw, k, v, a, g, k_k, k_a, r_k, lnx_w, lnx_b):
    seq, d_mix = r.shape
    assert LANES == SCAN_CHUNK and d_mix % LANES == 0
    rows = min(SCAN_CHUNK * SCAN_CHUNKS_PER_STEP, seq)
    assert seq % rows == 0 and rows % SCAN_CHUNK == 0
    act = pl.BlockSpec((rows, LANES), lambda p, c: (c, p))
    par = pl.BlockSpec((1, LANES), lambda p, c: (0, p))
    flat = lambda t: t.reshape(1, d_mix)
    return pl.pallas_call(
        _rwkv_scan_kernel,
        grid=(d_mix // LANES, seq // rows),
        in_specs=[act] * 6 + [par] * 5,
        out_specs=act,
        out_shape=jax.ShapeDtypeStruct((seq, d_mix), BF16),
        scratch_shapes=[pltpu.VMEM((LANES, LANES), F32)],
        compiler_params=_params("parallel", "arbitrary"),
    )(r, lw, k, v, a, g, flat(k_k), flat(k_a), flat(r_k), flat(lnx_w), flat(lnx_b))


def _norm_proj_kernel(n_proj, *refs):
    x_ref = refs[0]
    gains = refs[1:1 + n_proj]
    weights = refs[1 + n_proj:1 + 2 * n_proj]
    outs = refs[1 + 2 * n_proj:]
    xn = _rms_scale(x_ref[...])
    o = 0
    for g_ref, w_ref in zip(gains, weights):
        y = _bdot(xn * g_ref[...], w_ref[...])
        start = 0
        while start < y.shape[1]:
            width = outs[o].shape[1]
            outs[o][...] = y[:, start:start + width].astype(outs[o].dtype)
            start += width
            o += 1


def _norm_proj(x, projections):
    seq, d = x.shape
    tm = min(ROW_TILE, seq)
    row = lambda n: pl.BlockSpec((tm, n), lambda i: (i, 0))
    gains = [g.reshape(1, d) for g, _, _ in projections]
    weights = [w.astype(BF16) for _, w, _ in projections]
    widths = [n for _, _, ws in projections for n in ws]
    return pl.pallas_call(
        functools.partial(_norm_proj_kernel, len(projections)),
        grid=(seq // tm,),
        in_specs=([row(d)] + [_const_spec((1, d))] * len(gains)
                  + [_const_spec(w.shape) for w in weights]),
        out_specs=[row(n) for n in widths],
        out_shape=[jax.ShapeDtypeStruct((seq, n), BF16) for n in widths],
        compiler_params=_params("arbitrary"),
    )(x, *gains, *weights)


def _band_bias(table):
    ext = jnp.concatenate(
        [table, jnp.broadcast_to(table[:, -1:], (table.shape[0], BAND + CHUNK - 1 - table.shape[1]))],
        axis=1).astype(F32)
    rows = [ext[:, i:i + BAND] for i in range(CHUNK)]
    return jnp.stack(rows, axis=1)[:, :, ::-1]


def _chunk_attn_tile(first_tile, q_ref, kp_ref, kc_ref, vp_ref, vc_ref, bias_ref, o_ref):
    tq = q_ref.shape[0]
    k_win = jnp.concatenate([kp_ref[...], kc_ref[...]], axis=0)
    v_win = jnp.concatenate([vp_ref[...], vc_ref[...]], axis=0)
    bias = jnp.concatenate([bias_ref[hd] for hd in range(HEADS_PER_GROUP)], axis=0)
    lane = lax.broadcasted_iota(jnp.int32, (CHUNK, LANES), 1)
    col = lax.broadcasted_iota(jnp.int32, (HEADS_PER_GROUP * CHUNK, BAND), 1)
    scale = HEAD_DIM ** -0.5
    blocks = range(tq // CHUNK)
    scores = []
    for c in blocks:
        q = q_ref[c * CHUNK:(c + 1) * CHUNK, :] * scale
        q2 = jnp.concatenate(
            [jnp.where((lane // HEAD_DIM) == hd, q, jnp.zeros_like(q))
             for hd in range(HEADS_PER_GROUP)], axis=0)
        scores.append(_bdot(q2, k_win[c * CHUNK:c * CHUNK + BAND, :], NT))
    probs, denoms = [], []
    for c in blocks:
        s = scores[c] + bias
        if first_tile:
            s = jnp.where(col >= tq - c * CHUNK, s, MASK_VALUE)
        e = jnp.exp(s - jnp.max(s, axis=-1, keepdims=True))
        denoms.append(jnp.sum(e, axis=-1, keepdims=True))
        probs.append(e.astype(BF16))
    for c in blocks:
        o2 = _bdot(probs[c], v_win[c * CHUNK:c * CHUNK + BAND, :]) / denoms[c]
        out = o2[:CHUNK, :]
        for hd in range(1, HEADS_PER_GROUP):
            out = jnp.where((lane // HEAD_DIM) == hd, o2[hd * CHUNK:(hd + 1) * CHUNK, :], out)
        o_ref[c * CHUNK:(c + 1) * CHUNK, :] = out.astype(o_ref.dtype)


def _chunk_attn_kernel(*refs):
    is_first = pl.program_id(1) == 0
    pl.when(is_first)(functools.partial(_chunk_attn_tile, True, *refs))
    pl.when(jnp.logical_not(is_first))(functools.partial(_chunk_attn_tile, False, *refs))


def _chunk_attn(q, k, v, rel_table):
    seq, d_mix = q.shape
    tq = ATTN_TILE
    assert seq % tq == 0
    cur = pl.BlockSpec((tq, LANES), lambda p, i: (i, p))
    prev = pl.BlockSpec((tq, LANES), lambda p, i: (jnp.maximum(i - 1, 0), p))
    bias = _band_bias(rel_table)
    return pl.pallas_call(
        _chunk_attn_kernel,
        grid=(d_mix // LANES, seq // tq),
        in_specs=[cur, prev, cur, prev, cur,
                  pl.BlockSpec((HEADS_PER_GROUP, CHUNK, BAND), lambda p, i: (p, 0, 0))],
        out_specs=cur,
        out_shape=jax.ShapeDtypeStruct((seq, d_mix), BF16),
        compiler_params=_params("parallel", "arbitrary"),
    )(q, k, k, v, v, bias)


def _layer_post_kernel(has_final, d_mix, d_ff, *refs):
    refs = list(refs)
    (x_ref, mix_ref, qm_ref, kvm_ref, wout_ref, ln2_ref, fin_ref, cw_ref, cb_ref,
     fout_ref) = refs[:10]
    refs = refs[10:]
    lnf_ref = refs.pop(0) if has_final else None
    out_ref, carry = refs

    @pl.when(pl.program_id(0) == 0)
    def _():
        carry[...] = jnp.zeros_like(carry)

    tm = x_ref.shape[0]
    d_memq = qm_ref.shape[1]
    mix = mix_ref[...]

    qm = qm_ref[...] * HEAD_DIM ** -0.5
    km = kvm_ref[:, :d_memq]
    vm = kvm_ref[:, d_memq:]
    lane = lax.broadcasted_iota(jnp.int32, qm.shape, 1)
    head_masks = [(lane // HEAD_DIM) == hd for hd in range(d_memq // HEAD_DIM)]
    scores = [_bdot(jnp.where(mh, qm, jnp.zeros_like(qm)), km, NT) for mh in head_masks]
    proj_mix = _bdot(mix, wout_ref[:d_mix, :])
    exps = [jnp.exp(s - jnp.max(s, axis=-1, keepdims=True)) for s in scores]
    heads_out = [_bdot(e, vm) / jnp.sum(e, axis=-1, keepdims=True) for e in exps]
    mo = heads_out[0]
    for mh, o_h in zip(head_masks[1:], heads_out[1:]):
        mo = jnp.where(mh, o_h, mo)
    x1 = x_ref[...] + (proj_mix + _bdot(mo, wout_ref[d_mix:, :]))

    h2 = (_rms_scale(x1) * ln2_ref[...]).astype(BF16)

    def in_dots(c0):
        return (jnp.dot(h2, fin_ref[:, c0:c0 + FFN_COLS], preferred_element_type=F32),
                jnp.dot(h2, fin_ref[:, d_ff + c0:d_ff + c0 + FFN_COLS],
                        preferred_element_type=F32))

    slabs = list(range(0, d_ff, FFN_COLS))
    ffn = None
    nxt = in_dots(slabs[0])
    for idx, c0 in enumerate(slabs):
        cols = slice(c0, c0 + FFN_COLS)
        gate, val = nxt
        if idx + 1 < len(slabs):
            nxt = in_dots(slabs[idx + 1])
        prev = carry[SUBLANES - (CONV_W - 1):, cols]
        carry[:, cols] = gate[tm - SUBLANES:, :]
        conv = cb_ref[:, cols] + cw_ref[CONV_W - 1:CONV_W, cols] * gate
        for j in range(1, CONV_W):
            conv = conv + (cw_ref[CONV_W - 1 - j:CONV_W - j, cols]
                           * _shift_rows(gate, prev[CONV_W - 1 - j:, :], j))
        act = 0.5 * conv * (1.0 + lax.erf(conv * (2.0 ** -0.5))) * val
        slab = _bdot(act, fout_ref[cols, :])
        ffn = slab if ffn is None else ffn + slab
    acc = x1 + ffn
    if has_final:
        acc = _rms_scale(acc) * lnf_ref[...]
    out_ref[...] = acc


def _layer_post(x, mix, qm, layer, kv_mem, w_out, ln2, ffn_in, conv_w, conv_b, ffn_out, ln_f):
    seq, d = x.shape
    d_mix = mix.shape[1]
    d_memq = qm.shape[1]
    d_ff = ffn_out.shape[1]
    assert d_ff % FFN_COLS == 0 and conv_w.shape[0] == CONV_W
    tm = min(POST_TILE, seq)
    row = lambda n: pl.BlockSpec((tm, n), lambda i: (i, 0))
    ins, specs = [x, mix], [row(d), row(d_mix)]
    ins += [qm, kv_mem, w_out, ln2.reshape(1, d), ffn_in, conv_w, conv_b.reshape(1, d_ff),
            ffn_out]
    specs += [row(d_memq), _layer_spec(kv_mem.shape, layer), _layer_spec(w_out.shape, layer),
              _const_spec((1, d)), _layer_spec(ffn_in.shape, layer), _const_spec(conv_w.shape),
              _const_spec((1, d_ff)), _layer_spec(ffn_out.shape, layer)]
    if ln_f is not None:
        ins.append(ln_f.reshape(1, d))
        specs.append(_const_spec((1, d)))
    return pl.pallas_call(
        functools.partial(_layer_post_kernel, ln_f is not None, d_mix, d_ff),
        grid=(seq // tm,),
        in_specs=specs,
        out_specs=row(d),
        out_shape=jax.ShapeDtypeStruct((seq, d), F32),
        scratch_shapes=[pltpu.VMEM((SUBLANES, d_ff), F32)],
        compiler_params=_params("arbitrary"),
    )(*ins)


def kernel(x, mem, mem_norm, ln1, ln2, w_out, w_mem_kv, ffn_in, ffn_conv, ffn_conv_b, ffn_out, a_w_in, a_mu_rkv, a_mu_x, a_w0, a_w1, a_w2, a_a0, a_a1, a_a2, a_g1, a_g2, a_k_k, a_k_a, a_r_k, a_lnx_w, a_lnx_b, a_mu_v, a_v0, a_v1, a_v2, ln_kv, w_kv, b_w_in, b_rel, ln_f):
    bsz, seq, d = x.shape
    assert bsz == 1 and mem.shape[0] == 1
    depth = ln1.shape[0]
    n_a = a_w_in.shape[0]
    d_mix = a_w0.shape[1]
    xs = x.reshape(seq, d)
    kv_mem = _mem_kv(mem.reshape(mem.shape[1], d), mem_norm, w_mem_kv)
    a_w_in_b, w_out_b = a_w_in.astype(BF16), w_out.astype(BF16)
    ffn_in_b, ffn_out_b = ffn_in.astype(BF16), ffn_out.astype(BF16)

    v_first = None
    k_s = v_s = None
    for layer in range(depth):
        last = ln_f if layer == depth - 1 else None
        if layer < n_a:
            i = layer
            vres = None if i == 0 else (a_mu_v[i - 1], a_v0[i - 1], a_v1[i - 1], a_v2[i - 1])
            r, lw, k, v, a, g, qm = _rwkv_pre(
                xs, ln1[layer], a_w_in_b, i, a_mu_rkv[i], a_mu_x[i], a_w0[i], a_w1[i], a_w2[i],
                a_a0[i], a_a1[i], a_a2[i], a_g1[i], a_g2[i], vres, v_first)
            if i == 0:
                v_first = v
            mix = _rwkv_scan(r, lw, k, v, a, g, a_k_k[i], a_k_a[i], a_r_k[i],
                             a_lnx_w[i], a_lnx_b[i])
        else:
            j = layer - n_a
            d_memq = b_w_in.shape[2] - d_mix
            projections = [(ln1[layer], b_w_in[j], [d_mix, d_memq])]
            if j == 0:
                projections.append((ln_kv, w_kv, [d_mix, d_mix]))
            outs = _norm_proj(xs, projections)
            q, qm = outs[:2]
            if j == 0:
                k_s, v_s = outs[2:]
            mix = _chunk_attn(q, k_s, v_s, b_rel[j])
        xs = _layer_post(xs, mix, qm, layer, kv_mem, w_out_b, ln2[layer], ffn_in_b,
                         ffn_conv[layer], ffn_conv_b[layer], ffn_out_b, last)
    return xs.reshape(bsz, seq, d)
```

```python
import functools
import math

import jax
import jax.numpy as jnp
from jax import lax
from jax.experimental import pallas as pl
from jax.experimental.pallas import tpu as pltpu

F32 = jnp.float32
BF16 = jnp.bfloat16

HEAD_DIM = 64
LANES = 128
HEADS_PER_GROUP = LANES // HEAD_DIM
SUBLANES = 8
CHUNK = 64
LEFT_CHUNKS = 8
BAND = (LEFT_CHUNKS + 1) * CHUNK
REL_CLIP = 256
CONV_W = 3
LN_X_EPS = 64e-5
RMS_EPS = 1e-6
MASK_VALUE = -1e30

SCAN_CHUNK = 128
SCAN_CHUNKS_PER_STEP = 8
ROW_TILE = 512
POST_TILE = 1024
PRE_SUBTILES = 2
ATTN_TILE = LEFT_CHUNKS * CHUNK
FFN_COLS = 256
VMEM_LIMIT = 56 * 1024 * 1024

NT = (((1,), (1,)), ((), ()))
TN = (((0,), (0,)), ((), ()))
NN = (((1,), (0,)), ((), ()))


def _bdot(a, b, dims=NN):
    return lax.dot_general(a.astype(BF16), b.astype(BF16), dims,
                           preferred_element_type=F32)


def _split(x):
    hi = x.astype(BF16)
    lo = (x - hi.astype(F32)).astype(BF16)
    return hi, lo


def _rms_scale(x):
    return x * lax.rsqrt(jnp.mean(x * x, axis=-1, keepdims=True) + RMS_EPS)


def _shift_rows(t, prev_rows, n):
    rolled = pltpu.roll(t, n, 0)
    row = lax.broadcasted_iota(jnp.int32, t.shape, 0)
    out = rolled
    for j in range(n):
        out = jnp.where(row == j, prev_rows[j:j + 1, :], out)
    return out


def _const_spec(shape):
    nd = len(shape)
    return pl.BlockSpec(shape, lambda *_: (0,) * nd, pipeline_mode=pl.Buffered(1))


def _layer_spec(stacked_shape, layer):
    nd = len(stacked_shape) - 1
    return pl.BlockSpec((None,) + tuple(stacked_shape[1:]), lambda *_: (layer,) + (0,) * nd,
                        pipeline_mode=pl.Buffered(1))


def _params(*sem):
    return pltpu.CompilerParams(dimension_semantics=sem, vmem_limit_bytes=VMEM_LIMIT)


def _mem_kv_kernel(mem_ref, g_ref, w_ref, out_ref):
    mem_n = _rms_scale(mem_ref[...]) * g_ref[...]
    out_ref[...] = _bdot(mem_n, w_ref[...]).astype(out_ref.dtype)


def _mem_kv(mem, mem_norm, w_mem_kv):
    depth, d, n = w_mem_kv.shape
    n_mem = mem.shape[0]
    return pl.pallas_call(
        _mem_kv_kernel,
        grid=(depth,),
        in_specs=[pl.BlockSpec((n_mem, d), lambda l: (0, 0)),
                  pl.BlockSpec((1, d), lambda l: (0, 0)),
                  pl.BlockSpec((None, d, n), lambda l: (l, 0, 0))],
        out_specs=pl.BlockSpec((None, n_mem, n), lambda l: (l, 0, 0)),
        out_shape=jax.ShapeDtypeStruct((depth, n_mem, n), BF16),
        compiler_params=_params("arbitrary"),
    )(mem, mem_norm.reshape(1, d), w_mem_kv.astype(BF16))


def _rwkv_pre_kernel(has_vres, d_mix, *refs):
    if has_vres:
        (x_ref, ln_ref, win_ref, wlh_ref, wld_ref, w2_ref, a2_ref, g2_ref, mu_ref,
         w0_ref, a0_ref, v2_ref, v0_ref, vf_ref,
         r_ref, lw_ref, k_ref, v_ref, a_ref, g_ref, qm_ref, hprev, pprev) = refs
    else:
        (x_ref, ln_ref, win_ref, wlh_ref, wld_ref, w2_ref, a2_ref, g2_ref, mu_ref,
         w0_ref, a0_ref,
         r_ref, lw_ref, k_ref, v_ref, a_ref, g_ref, qm_ref, hprev, pprev) = refs

    @pl.when(pl.program_id(0) == 0)
    def _():
        hprev[...] = jnp.zeros_like(hprev)
        pprev[...] = jnp.zeros_like(pprev)

    sub = x_ref.shape[0] // PRE_SUBTILES
    tiles = [slice(s * sub, (s + 1) * sub) for s in range(PRE_SUBTILES)]
    h_last = hprev[...]
    first = []
    for rows in tiles:
        h = _rms_scale(x_ref[rows, :]) * ln_ref[...]
        dh = _shift_rows(h, h_last, 1) - h
        h_last = h[sub - 1:sub, :]
        hb = h.astype(BF16)
        p = jnp.dot(hb, win_ref[...], preferred_element_type=F32)
        l1 = (jnp.dot(hb, wlh_ref[...], preferred_element_type=F32)
              + jnp.dot(dh.astype(BF16), wld_ref[...], preferred_element_type=F32))
        first.append((p, l1))
    hprev[...] = h_last

    mu = mu_ref[...]
    p_last = pprev[...]
    for rows, (p, l1) in zip(tiles, first):
        prkv = p[:, :3 * d_mix]
        ps = _shift_rows(prkv, p_last, 1)
        p_last = prkv[sub - 1:sub, :]

        def lerp(j):
            cur = prkv[:, j * d_mix:(j + 1) * d_mix]
            return cur + (ps[:, j * d_mix:(j + 1) * d_mix] - cur) * mu[j:j + 1, :]

        r_ref[rows, :] = lerp(0)
        k_ref[rows, :] = lerp(1)
        v = lerp(2)
        qm_ref[rows, :] = p[:, 3 * d_mix:].astype(qm_ref.dtype)

        l_wa = l1[:, :LANES]
        z = w0_ref[...] + _bdot(jnp.tanh(l_wa), w2_ref[...])
        lw_ref[rows, :] = -math.exp(-0.5) * jax.nn.sigmoid(z)
        a_ref[rows, :] = jax.nn.sigmoid(a0_ref[...] + _bdot(l_wa, a2_ref[...]))
        g_ref[rows, :] = _bdot(jax.nn.sigmoid(l1[:, LANES:2 * LANES]), g2_ref[...])
        if has_vres:
            gate = jax.nn.sigmoid(v0_ref[...]
                                  + _bdot(l1[:, 2 * LANES:3 * LANES], v2_ref[...]))
            v = v + (vf_ref[rows, :] - v) * gate
        v_ref[rows, :] = v
    pprev[...] = p_last


def _pad_rows(w, rows, offset):
    out = jnp.zeros((rows, w.shape[1]), w.dtype)
    return out.at[offset:offset + w.shape[0]].set(w)


def _rwkv_pre(x, ln1, w_in_all, layer, mu_rkv, mu_x, w0, w1, w2, a0, a1, a2, g1, g2, vres,
              v_first):
    seq, d = x.shape
    d_mix = w0.shape[0]
    d_memq = w_in_all.shape[2] - 3 * d_mix
    lora_w, lora_a, lora_g = w1.shape[1], a1.shape[1], g1.shape[1]
    assert lora_w + lora_a == LANES and lora_g == LANES
    has_vres = vres is not None
    firsts = [w1, a1, g1]
    mus = [mu_x[0], mu_x[1], mu_x[2]]
    if has_vres:
        mu_v, v0, v1, v2 = vres
        assert v1.shape[1] <= LANES
        firsts.append(jnp.pad(v1, ((0, 0), (0, LANES - v1.shape[1]))))
        mus.append(mu_v)
    wl_h = jnp.concatenate(firsts, axis=1)
    wl_d = jnp.concatenate([m[:, None] * w for m, w in zip(mus, firsts)], axis=1)
    lp = wl_h.shape[1]
    w2p = _pad_rows(w2, LANES, 0)
    a2p = _pad_rows(a2, LANES, lora_w)

    tm = min(ROW_TILE, seq)
    assert seq % tm == 0
    row = lambda n: pl.BlockSpec((tm, n), lambda i: (i, 0))
    ins = [x, ln1.reshape(1, d), w_in_all, wl_h.astype(BF16), wl_d.astype(BF16),
           w2p.astype(BF16), a2p.astype(BF16), g2.astype(BF16), mu_rkv,
           w0.reshape(1, d_mix), a0.reshape(1, d_mix)]
    specs = [row(d), _const_spec((1, d)), _layer_spec(w_in_all.shape, layer),
             _const_spec((d, lp)),
             _const_spec((d, lp)), _const_spec((LANES, d_mix)), _const_spec((LANES, d_mix)),
             _const_spec((LANES, d_mix)), _const_spec(mu_rkv.shape),
             _const_spec((1, d_mix)), _const_spec((1, d_mix))]
    if has_vres:
        ins += [_pad_rows(v2, LANES, 0).astype(BF16), v0.reshape(1, d_mix), v_first]
        specs += [_const_spec((LANES, d_mix)), _const_spec((1, d_mix)), row(d_mix)]
    mix_out = jax.ShapeDtypeStruct((seq, d_mix), F32)
    return pl.pallas_call(
        functools.partial(_rwkv_pre_kernel, has_vres, d_mix),
        grid=(seq // tm,),
        in_specs=specs,
        out_specs=[row(d_mix)] * 6 + [row(d_memq)],
        out_shape=[mix_out] * 6 + [jax.ShapeDtypeStruct((seq, d_memq), BF16)],
        scratch_shapes=[pltpu.VMEM((1, d), F32), pltpu.VMEM((1, 3 * d_mix), F32)],
        compiler_params=_params("arbitrary"),
    )(*ins)


def _scan_chunks(chunks, gates, kk_w, ka_w, rk_w, lnw, lnb, state, consts):
    tri_incl_b, strict, incl, eye, same_head, head0 = consts
    n = SCAN_CHUNK
    heads = range(HEADS_PER_GROUP)
    head_mask = [head0, jnp.logical_not(head0)]
    nc = len(chunks)

    def head_sums(xs):
        out = []
        for x in xs:
            s0 = jnp.sum(jnp.where(head0, x, 0.0), axis=-1, keepdims=True)
            s1 = jnp.sum(jnp.where(head0, 0.0, x), axis=-1, keepdims=True)
            out.append(jnp.where(head0, s0, s1))
        return out

    kk = [k * kk_w for (_, _, k, _, _) in chunks]
    kmod = [k * (1.0 + (a - 1.0) * ka_w) for (_, _, k, _, a) in chunks]
    sums = head_sums([x * x for x in kk] + [chunks[j][0] * kmod[j] * rk_w for j in range(nc)])
    ss, bonus = sums[:nc], sums[nc:]
    tri2 = jnp.concatenate([tri_incl_b, tri_incl_b], axis=1)
    cum = [jnp.dot(tri2, jnp.concatenate(_split(lw), axis=0), preferred_element_type=F32)
           for (_, lw, _, _, _) in chunks]

    pre = []
    for j, (r, lw, k, v, a) in enumerate(chunks):
        kkn = kk[j] / jnp.maximum(jnp.sqrt(ss[j]), 1e-12)
        c = cum[j]
        ref = c[n // 2 - 1:n // 2, :]
        inv = jnp.exp(ref - c)
        rt = r * jnp.exp(c - ref)
        at = -kkn * jnp.exp(c - lw - ref)
        lhs = [jnp.where(m, x, 0.0).astype(BF16) for m in head_mask for x in (at, rt)]
        pre.append(dict(
            ref=ref, e_last=jnp.exp(c[n - 1:n, :] - ref), rt=rt, at=at.astype(BF16),
            vb=v.astype(BF16), lhs=jnp.concatenate(lhs, axis=0),
            bk=jnp.concatenate([(kkn * a * inv).astype(BF16), (kmod[j] * inv).astype(BF16)],
                               axis=0)))

    quad = [_bdot(pre[j]["lhs"], pre[j]["bk"], NT) for j in range(nc)]
    chains = [(j, h) for j in range(nc) for h in heads]
    strict2 = jnp.concatenate([strict, strict], axis=1)
    incl2 = jnp.concatenate([incl, incl], axis=1)
    ab_ak = {(j, h): jnp.where(strict2, quad[j][2 * h * n:(2 * h + 1) * n], 0.0)
             for j, h in chains}
    rb_rk = {(j, h): jnp.where(incl2, quad[j][(2 * h + 1) * n:(2 * h + 2) * n], 0.0).astype(BF16)
             for j, h in chains}

    half = n // 2
    left = lax.broadcasted_iota(jnp.int32, (half, n), 1) < half

    def block_diag(packed):
        return jnp.concatenate([jnp.where(left, packed, jnp.zeros_like(packed)),
                                jnp.where(left, jnp.zeros_like(packed), packed)], axis=0)

    eye_packed = jnp.where(left, eye[:half], eye[half:])
    nmat = {ch: ab_ak[ch][:, :n] for ch in chains}
    packed = {ch: jnp.where(left, nmat[ch][:half], nmat[ch][half:]) for ch in chains}
    t_p = {ch: eye_packed + packed[ch] for ch in chains}
    pb = {ch: packed[ch].astype(BF16) for ch in chains}
    power = {ch: _bdot(pb[ch], block_diag(pb[ch])).astype(BF16) for ch in chains}
    for _ in range(int(math.log2(half)) - 2):
        both = {ch: _bdot(jnp.concatenate([t_p[ch].astype(BF16), power[ch]], axis=0),
                          block_diag(power[ch])) for ch in chains}
        t_p = {ch: t_p[ch] + both[ch][:half] for ch in chains}
        power = {ch: both[ch][half:].astype(BF16) for ch in chains}
    t_p = {ch: t_p[ch] + _bdot(t_p[ch], block_diag(power[ch])) for ch in chains}
    tp_b = {ch: t_p[ch].astype(BF16) for ch in chains}
    n21_t1 = {ch: _bdot(jnp.where(left, nmat[ch][half:], 0.0), block_diag(tp_b[ch]))
              for ch in chains}
    zeros_half = jnp.zeros((half, n), BF16)
    t21 = {ch: _bdot(jnp.where(left, jnp.zeros_like(tp_b[ch]), tp_b[ch]),
                     jnp.concatenate([zeros_half, n21_t1[ch].astype(BF16)], axis=0))
           for ch in chains}
    t_b = {ch: jnp.concatenate(
        [jnp.where(left, t_p[ch], 0.0),
         t21[ch] + jnp.where(left, 0.0, t_p[ch])], axis=0).astype(BF16) for ch in chains}

    def by_head(x):
        wide = (lax.broadcasted_iota(jnp.int32, x.shape, 1) % LANES) < HEAD_DIM
        return jnp.concatenate([jnp.where(wide, x, jnp.zeros_like(x)),
                                jnp.where(wide, jnp.zeros_like(x), x)], axis=0)

    def heads_on_k(mats, j):
        return jnp.concatenate([mats[(j, h)] for h in heads], axis=1)

    ak_b = {ch: ab_ak[ch][:, n:].astype(BF16) for ch in chains}
    akv = [_bdot(heads_on_k(ak_b, j), by_head(pre[j]["vb"])).astype(BF16) for j in range(nc)]
    t_x = [_bdot(heads_on_k(t_b, j), by_head(jnp.concatenate([pre[j]["at"], akv[j]], axis=1)))
           for j in range(nc)]
    ahat = [t[:, :LANES].astype(BF16) for t in t_x]
    uin = [t[:, LANES:].astype(BF16) for t in t_x]
    zero = jnp.zeros((n, LANES), BF16)
    r_x = [_bdot(heads_on_k(rb_rk, j),
                 by_head(jnp.concatenate([jnp.concatenate([ahat[j], uin[j]], axis=1),
                                          jnp.concatenate([zero, pre[j]["vb"]], axis=1)], axis=0)))
           for j in range(nc)]
    rhat = [pre[j]["rt"] + r_x[j][:, :LANES] for j in range(nc)]
    yin = [r_x[j][:, LANES:] for j in range(nc)]
    m_mat = [(eye + jnp.where(same_head, _bdot(ahat[j], pre[j]["bk"][:n], TN), 0.0))
             * pre[j]["e_last"] for j in range(nc)]
    c_mat = [jnp.where(same_head,
                       _bdot(jnp.concatenate([uin[j], pre[j]["vb"]], axis=0), pre[j]["bk"], TN),
                       0.0) * pre[j]["e_last"] for j in range(nc)]

    ys = []
    for j in range(nc):
        sp = state * jnp.exp(pre[j]["ref"])
        ys.append(_bdot(rhat[j], sp, NT) + yin[j])
        state = _bdot(sp, m_mat[j]) + c_mat[j]

    inv_hd = 1.0 / HEAD_DIM
    mean = [m * inv_hd for m in head_sums(ys)]
    dy = [y - m for y, m in zip(ys, mean)]
    var = [s * inv_hd for s in head_sums([d * d for d in dy])]
    outs = [(dy[j] * lax.rsqrt(var[j] + LN_X_EPS) * lnw + lnb + bonus[j] * chunks[j][3])
            * gates[j] for j in range(nc)]
    return outs, state


def _rwkv_scan_kernel(r_ref, lw_ref, k_ref, v_ref, a_ref, g_ref, kk_ref, ka_ref, rk_ref,
                      lnw_ref, lnb_ref, y_ref, state_ref):
    @pl.when(pl.program_id(1) == 0)
    def _():
        state_ref[...] = jnp.zeros_like(state_ref)

    n = SCAN_CHUNK
    row = lax.broadcasted_iota(jnp.int32, (n, n), 0)
    col = lax.broadcasted_iota(jnp.int32, (n, n), 1)
    incl = row >= col
    strict = row > col
    same_head = (row // HEAD_DIM) == (col // HEAD_DIM)
    consts = (incl.astype(BF16), strict, incl, (row == col).astype(F32), same_head,
              col < HEAD_DIM)

    nc = r_ref.shape[0] // n
    chunks = [tuple(ref[j * n:(j + 1) * n, :] for ref in (r_ref, lw_ref, k_ref, v_ref, a_ref))
              for j in range(nc)]
    gates = [g_ref[j * n:(j + 1) * n, :] for j in range(nc)]
    outs, state = _scan_chunks(chunks, gates, kk_ref[...], ka_ref[...], rk_ref[...],
                               lnw_ref[...], lnb_ref[...], state_ref[...], consts)
    for j in range(nc):
        y_ref[j * n:(j + 1) * n, :] = outs[j].astype(y_ref.dtype)
    state_ref[...] = state


def _rwkv_scan(r, lw, k, v, a, g, k_k, k_a, r_k, lnx_w, lnx_b):
    seq, d_mix = r.shape
    assert LANES == SCAN_CHUNK and d_mix % LANES == 0
    rows = min(SCAN_CHUNK * SCAN_CHUNKS_PER_STEP, seq)
    assert seq % rows == 0 and rows % SCAN_CHUNK == 0
    act = pl.BlockSpec((rows, LANES), lambda p, c: (c, p))
    par = pl.BlockSpec((1, LANES), lambda p, c: (0, p))
    flat = lambda t: t.reshape(1, d_mix)
    return pl.pallas_call(
        _rwkv_scan_kernel,
        grid=(d_mix // LANES, seq // rows),
        in_specs=[act] * 6 + [par] * 5,
        out_specs=act,
        out_shape=jax.ShapeDtypeStruct((seq, d_mix), BF16),
        scratch_shapes=[pltpu.VMEM((LANES, LANES), F32)],
        compiler_params=_params("parallel", "arbitrary"),
    )(r, lw, k, v, a, g, flat(k_k), flat(k_a), flat(r_k), flat(lnx_w), flat(lnx_b))


def _norm_proj_kernel(n_proj, *refs):
    x_ref = refs[0]
    gains = refs[1:1 + n_proj]
    weights = refs[1 + n_proj:1 + 2 * n_proj]
    outs = refs[1 + 2 * n_proj:]
    xn = _rms_scale(x_ref[...])
    o = 0
    for g_ref, w_ref in zip(gains, weights):
        y = _bdot(xn * g_ref[...], w_ref[...])
        start = 0
        while start < y.shape[1]:
            width = outs[o].shape[1]
            outs[o][...] = y[:, start:start + width].astype(outs[o].dtype)
            start += width
            o += 1


def _norm_proj(x, projections):
    seq, d = x.shape
    tm = min(ROW_TILE, seq)
    row = lambda n: pl.BlockSpec((tm, n), lambda i: (i, 0))
    gains = [g.reshape(1, d) for g, _, _ in projections]
    weights = [w.astype(BF16) for _, w, _ in projections]
    widths = [n for _, _, ws in projections for n in ws]
    return pl.pallas_call(
        functools.partial(_norm_proj_kernel, len(projections)),
        grid=(seq // tm,),
        in_specs=([row(d)] + [_const_spec((1, d))] * len(gains)
                  + [_const_spec(w.shape) for w in weights]),
        out_specs=[row(n) for n in widths],
        out_shape=[jax.ShapeDtypeStruct((seq, n), BF16) for n in widths],
        compiler_params=_params("arbitrary"),
    )(x, *gains, *weights)


def _band_bias(table):
    ext = jnp.concatenate(
        [table, jnp.broadcast_to(table[:, -1:], (table.shape[0], BAND + CHUNK - 1 - table.shape[1]))],
        axis=1).astype(F32)
    rows = [ext[:, i:i + BAND] for i in range(CHUNK)]
    return jnp.stack(rows, axis=1)[:, :, ::-1]


def _chunk_attn_tile(first_tile, q_ref, kp_ref, kc_ref, vp_ref, vc_ref, bias_ref, o_ref):
    tq = q_ref.shape[0]
    k_win = jnp.concatenate([kp_ref[...], kc_ref[...]], axis=0)
    v_win = jnp.concatenate([vp_ref[...], vc_ref[...]], axis=0)
    bias = jnp.concatenate([bias_ref[hd] for hd in range(HEADS_PER_GROUP)], axis=0)
    lane = lax.broadcasted_iota(jnp.int32, (CHUNK, LANES), 1)
    col = lax.broadcasted_iota(jnp.int32, (HEADS_PER_GROUP * CHUNK, BAND), 1)
    scale = HEAD_DIM ** -0.5
    blocks = range(tq // CHUNK)
    scores = []
    for c in blocks:
        q = q_ref[c * CHUNK:(c + 1) * CHUNK, :] * scale
        q2 = jnp.concatenate(
            [jnp.where((lane // HEAD_DIM) == hd, q, jnp.zeros_like(q))
             for hd in range(HEADS_PER_GROUP)], axis=0)
        scores.append(_bdot(q2, k_win[c * CHUNK:c * CHUNK + BAND, :], NT))
    probs, denoms = [], []
    for c in blocks:
        s = scores[c] + bias
        if first_tile:
            s = jnp.where(col >= tq - c * CHUNK, s, MASK_VALUE)
        e = jnp.exp(s - jnp.max(s, axis=-1, keepdims=True))
        denoms.append(jnp.sum(e, axis=-1, keepdims=True))
        probs.append(e.astype(BF16))
    for c in blocks:
        o2 = _bdot(probs[c], v_win[c * CHUNK:c * CHUNK + BAND, :]) / denoms[c]
        out = o2[:CHUNK, :]
        for hd in range(1, HEADS_PER_GROUP):
            out = jnp.where((lane // HEAD_DIM) == hd, o2[hd * CHUNK:(hd + 1) * CHUNK, :], out)
        o_ref[c * CHUNK:(c + 1) * CHUNK, :] = out.astype(o_ref.dtype)


def _chunk_attn_kernel(*refs):
    is_first = pl.program_id(1) == 0
    pl.when(is_first)(functools.partial(_chunk_attn_tile, True, *refs))
    pl.when(jnp.logical_not(is_first))(functools.partial(_chunk_attn_tile, False, *refs))


def _chunk_attn(q, k, v, rel_table):
    seq, d_mix = q.shape
    tq = ATTN_TILE
    assert seq % tq == 0
    cur = pl.BlockSpec((tq, LANES), lambda p, i: (i, p))
    prev = pl.BlockSpec((tq, LANES), lambda p, i: (jnp.maximum(i - 1, 0), p))
    bias = _band_bias(rel_table)
    return pl.pallas_call(
        _chunk_attn_kernel,
        grid=(d_mix // LANES, seq // tq),
        in_specs=[cur, prev, cur, prev, cur,
                  pl.BlockSpec((HEADS_PER_GROUP, CHUNK, BAND), lambda p, i: (p, 0, 0))],
        out_specs=cur,
        out_shape=jax.ShapeDtypeStruct((seq, d_mix), BF16),
        compiler_params=_params("parallel", "arbitrary"),
    )(q, k, k, v, v, bias)


def _layer_post_kernel(has_final, d_mix, d_ff, *refs):
    refs = list(refs)
    (x_ref, mix_ref, qm_ref, kvm_ref, wout_ref, ln2_ref, fin_ref, cw_ref, cb_ref,
     fout_ref) = refs[:10]
    refs = refs[10:]
    lnf_ref = refs.pop(0) if has_final else None
    out_ref, carry = refs

    @pl.when(pl.program_id(0) == 0)
    def _():
        carry[...] = jnp.zeros_like(carry)

    tm = x_ref.shape[0]
    d_memq = qm_ref.shape[1]
    mix = mix_ref[...]

    qm = qm_ref[...] * HEAD_DIM ** -0.5
    km = kvm_ref[:, :d_memq]
    vm = kvm_ref[:, d_memq:]
    lane = lax.broadcasted_iota(jnp.int32, qm.shape, 1)
    head_masks = [(lane // HEAD_DIM) == hd for hd in range(d_memq // HEAD_DIM)]
    scores = [_bdot(jnp.where(mh, qm, jnp.zeros_like(qm)), km, NT) for mh in head_masks]
    proj_mix = _bdot(mix, wout_ref[:d_mix, :])
    exps = [jnp.exp(s - jnp.max(s, axis=-1, keepdims=True)) for s in scores]
    heads_out = [_bdot(e, vm) / jnp.sum(e, axis=-1, keepdims=True) for e in exps]
    mo = heads_out[0]
    for mh, o_h in zip(head_masks[1:], heads_out[1:]):
        mo = jnp.where(mh, o_h, mo)
    x1 = x_ref[...] + (proj_mix + _bdot(mo, wout_ref[d_mix:, :]))

    h2 = (_rms_scale(x1) * ln2_ref[...]).astype(BF16)

    def in_dots(c0):
        return (jnp.dot(h2, fin_ref[:, c0:c0 + FFN_COLS], preferred_element_type=F32),
                jnp.dot(h2, fin_ref[:, d_ff + c0:d_ff + c0 + FFN_COLS],
                        preferred_element_type=F32))

    slabs = list(range(0, d_ff, FFN_COLS))
    ffn = None
    nxt = in_dots(slabs[0])
    for idx, c0 in enumerate(slabs):
        cols = slice(c0, c0 + FFN_COLS)
        gate, val = nxt
        if idx + 1 < len(slabs):
            nxt = in_dots(slabs[idx + 1])
        prev = carry[SUBLANES - (CONV_W - 1):, cols]
        carry[:, cols] = gate[tm - SUBLANES:, :]
        conv = cb_ref[:, cols] + cw_ref[CONV_W - 1:CONV_W, cols] * gate
        for j in range(1, CONV_W):
            conv = conv + (cw_ref[CONV_W - 1 - j:CONV_W - j, cols]
                           * _shift_rows(gate, prev[CONV_W - 1 - j:, :], j))
        act = 0.5 * conv * (1.0 + lax.erf(conv * (2.0 ** -0.5))) * val
        slab = _bdot(act, fout_ref[cols, :])
        ffn = slab if ffn is None else ffn + slab
    acc = x1 + ffn
    if has_final:
        acc = _rms_scale(acc) * lnf_ref[...]
    out_ref[...] = acc


def _layer_post(x, mix, qm, layer, kv_mem, w_out, ln2, ffn_in, conv_w, conv_b, ffn_out, ln_f):
    seq, d = x.shape
    d_mix = mix.shape[1]
    d_memq = qm.shape[1]
    d_ff = ffn_out.shape[1]
    assert d_ff % FFN_COLS == 0 and conv_w.shape[0] == CONV_W
    tm = min(POST_TILE, seq)
    row = lambda n: pl.BlockSpec((tm, n), lambda i: (i, 0))
    ins, specs = [x, mix], [row(d), row(d_mix)]
    ins += [qm, kv_mem, w_out, ln2.reshape(1, d), ffn_in, conv_w, conv_b.reshape(1, d_ff),
            ffn_out]
    specs += [row(d_memq), _layer_spec(kv_mem.shape, layer), _layer_spec(w_out.shape, layer),
              _const_spec((1, d)), _layer_spec(ffn_in.shape, layer), _const_spec(conv_w.shape),
              _const_spec((1, d_ff)), _layer_spec(ffn_out.shape, layer)]
    if ln_f is not None:
        ins.append(ln_f.reshape(1, d))
        specs.append(_const_spec((1, d)))
    return pl.pallas_call(
        functools.partial(_layer_post_kernel, ln_f is not None, d_mix, d_ff),
        grid=(seq // tm,),
        in_specs=specs,
        out_specs=row(d),
        out_shape=jax.ShapeDtypeStruct((seq, d), F32),
        scratch_shapes=[pltpu.VMEM((SUBLANES, d_ff), F32)],
        compiler_params=_params("arbitrary"),
    )(*ins)


def kernel(x, mem, mem_norm, ln1, ln2, w_out, w_mem_kv, ffn_in, ffn_conv, ffn_conv_b, ffn_out, a_w_in, a_mu_rkv, a_mu_x, a_w0, a_w1, a_w2, a_a0, a_a1, a_a2, a_g1, a_g2, a_k_k, a_k_a, a_r_k, a_lnx_w, a_lnx_b, a_mu_v, a_v0, a_v1, a_v2, ln_kv, w_kv, b_w_in, b_rel, ln_f):
    bsz, seq, d = x.shape
    assert bsz == 1 and mem.shape[0] == 1
    depth = ln1.shape[0]
    n_a = a_w_in.shape[0]
    d_mix = a_w0.shape[1]
    xs = x.reshape(seq, d)
    kv_mem = _mem_kv(mem.reshape(mem.shape[1], d), mem_norm, w_mem_kv)
    a_w_in_b, w_out_b = a_w_in.astype(BF16), w_out.astype(BF16)
    ffn_in_b, ffn_out_b = ffn_in.astype(BF16), ffn_out.astype(BF16)

    v_first = None
    k_s = v_s = None
    for layer in range(depth):
        last = ln_f if layer == depth - 1 else None
        if layer < n_a:
            i = layer
            vres = None if i == 0 else (a_mu_v[i - 1], a_v0[i - 1], a_v1[i - 1], a_v2[i - 1])
            r, lw, k, v, a, g, qm = _rwkv_pre(
                xs, ln1[layer], a_w_in_b, i, a_mu_rkv[i], a_mu_x[i], a_w0[i], a_w1[i], a_w2[i],
                a_a0[i], a_a1[i], a_a2[i], a_g1[i], a_g2[i], vres, v_first)
            if i == 0:
                v_first = v
            mix = _rwkv_scan(r, lw, k, v, a, g, a_k_k[i], a_k_a[i], a_r_k[i],
                             a_lnx_w[i], a_lnx_b[i])
        else:
            j = layer - n_a
            d_memq = b_w_in.shape[2] - d_mix
            projections = [(ln1[layer], b_w_in[j], [d_mix, d_memq])]
            if j == 0:
                projections.append((ln_kv, w_kv, [d_mix, d_mix]))
            outs = _norm_proj(xs, projections)
            q, qm = outs[:2]
            if j == 0:
                k_s, v_s = outs[2:]
            mix = _chunk_attn(q, k_s, v_s, b_rel[j])
        xs = _layer_post(xs, mix, qm, layer, kv_mem, w_out_b, ln2[layer], ffn_in_b,
                         ffn_conv[layer], ffn_conv_b[layer], ffn_out_b, last)
    return xs.reshape(bsz, seq, d)
```

```python
import functools
import math

import jax
import jax.numpy as jnp
from jax import lax
from jax.experimental import pallas as pl
from jax.experimental.pallas import tpu as pltpu

F32 = jnp.float32
BF16 = jnp.bfloat16

HEAD_DIM = 64
LANES = 128
HEADS_PER_GROUP = LANES // HEAD_DIM
SUBLANES = 8
CHUNK = 64
LEFT_CHUNKS = 8
BAND = (LEFT_CHUNKS + 1) * CHUNK
REL_CLIP = 256
CONV_W = 3
LN_X_EPS = 64e-5
RMS_EPS = 1e-6
MASK_VALUE = -1e30

SCAN_CHUNK = 128
SCAN_CHUNKS_PER_STEP = 16
INV_BASE = 64
ROW_TILE = 512
POST_TILE = 1024
PRE_SUBTILES = 2
ATTN_TILE = LEFT_CHUNKS * CHUNK
FFN_COLS = 256
VMEM_LIMIT = 56 * 1024 * 1024

NT = (((1,), (1,)), ((), ()))
TN = (((0,), (0,)), ((), ()))
NN = (((1,), (0,)), ((), ()))


def _bdot(a, b, dims=NN):
    return lax.dot_general(a.astype(BF16), b.astype(BF16), dims,
                           preferred_element_type=F32)


def _split(x):
    hi = x.astype(BF16)
    lo = (x - hi.astype(F32)).astype(BF16)
    return hi, lo


def _rms_scale(x):
    return x * lax.rsqrt(jnp.mean(x * x, axis=-1, keepdims=True) + RMS_EPS)


def _shift_rows(t, prev_rows, n):
    rolled = pltpu.roll(t, n, 0)
    row = lax.broadcasted_iota(jnp.int32, t.shape, 0)
    out = rolled
    for j in range(n):
        out = jnp.where(row == j, prev_rows[j:j + 1, :], out)
    return out


def _const_spec(shape):
    nd = len(shape)
    return pl.BlockSpec(shape, lambda *_: (0,) * nd, pipeline_mode=pl.Buffered(1))


def _layer_spec(stacked_shape, layer):
    nd = len(stacked_shape) - 1
    return pl.BlockSpec((None,) + tuple(stacked_shape[1:]), lambda *_: (layer,) + (0,) * nd,
                        pipeline_mode=pl.Buffered(1))


def _params(*sem):
    return pltpu.CompilerParams(dimension_semantics=sem, vmem_limit_bytes=VMEM_LIMIT)


def _mem_kv_kernel(mem_ref, g_ref, w_ref, out_ref):
    mem_n = _rms_scale(mem_ref[...]) * g_ref[...]
    out_ref[...] = _bdot(mem_n, w_ref[...]).astype(out_ref.dtype)


def _mem_kv(mem, mem_norm, w_mem_kv):
    depth, d, n = w_mem_kv.shape
    n_mem = mem.shape[0]
    return pl.pallas_call(
        _mem_kv_kernel,
        grid=(depth,),
        in_specs=[pl.BlockSpec((n_mem, d), lambda l: (0, 0)),
                  pl.BlockSpec((1, d), lambda l: (0, 0)),
                  pl.BlockSpec((None, d, n), lambda l: (l, 0, 0))],
        out_specs=pl.BlockSpec((None, n_mem, n), lambda l: (l, 0, 0)),
        out_shape=jax.ShapeDtypeStruct((depth, n_mem, n), BF16),
        compiler_params=_params("arbitrary"),
    )(mem, mem_norm.reshape(1, d), w_mem_kv.astype(BF16))


def _rwkv_pre_kernel(has_vres, d_mix, *refs):
    if has_vres:
        (x_ref, ln_ref, win_ref, wlh_ref, wld_ref, w2_ref, a2_ref, g2_ref, mu_ref,
         w0_ref, a0_ref, v2_ref, v0_ref, vf_ref,
         r_ref, lw_ref, k_ref, v_ref, a_ref, g_ref, qm_ref, hprev, pprev) = refs
    else:
        (x_ref, ln_ref, win_ref, wlh_ref, wld_ref, w2_ref, a2_ref, g2_ref, mu_ref,
         w0_ref, a0_ref,
         r_ref, lw_ref, k_ref, v_ref, a_ref, g_ref, qm_ref, hprev, pprev) = refs

    @pl.when(pl.program_id(0) == 0)
    def _():
        hprev[...] = jnp.zeros_like(hprev)
        pprev[...] = jnp.zeros_like(pprev)

    sub = x_ref.shape[0] // PRE_SUBTILES
    tiles = [slice(s * sub, (s + 1) * sub) for s in range(PRE_SUBTILES)]
    h_last = hprev[...]
    first = []
    for rows in tiles:
        h = _rms_scale(x_ref[rows, :]) * ln_ref[...]
        dh = _shift_rows(h, h_last, 1) - h
        h_last = h[sub - 1:sub, :]
        hb = h.astype(BF16)
        p = jnp.dot(hb, win_ref[...], preferred_element_type=F32)
        l1 = (jnp.dot(hb, wlh_ref[...], preferred_element_type=F32)
              + jnp.dot(dh.astype(BF16), wld_ref[...], preferred_element_type=F32))
        first.append((p, l1))
    hprev[...] = h_last

    mu = mu_ref[...]
    p_last = pprev[...]
    for rows, (p, l1) in zip(tiles, first):
        prkv = p[:, :3 * d_mix]
        ps = _shift_rows(prkv, p_last, 1)
        p_last = prkv[sub - 1:sub, :]

        def lerp(j):
            cur = prkv[:, j * d_mix:(j + 1) * d_mix]
            return cur + (ps[:, j * d_mix:(j + 1) * d_mix] - cur) * mu[j:j + 1, :]

        r_ref[rows, :] = lerp(0)
        k_ref[rows, :] = lerp(1)
        v = lerp(2)
        qm_ref[rows, :] = p[:, 3 * d_mix:].astype(qm_ref.dtype)

        l_wa = l1[:, :LANES]
        z = w0_ref[...] + _bdot(jnp.tanh(l_wa), w2_ref[...])
        lw_ref[rows, :] = -math.exp(-0.5) * jax.nn.sigmoid(z)
        a_ref[rows, :] = jax.nn.sigmoid(a0_ref[...] + _bdot(l_wa, a2_ref[...]))
        g_ref[rows, :] = _bdot(jax.nn.sigmoid(l1[:, LANES:2 * LANES]), g2_ref[...])
        if has_vres:
            gate = jax.nn.sigmoid(v0_ref[...]
                                  + _bdot(l1[:, 2 * LANES:3 * LANES], v2_ref[...]))
            v = v + (vf_ref[rows, :] - v) * gate
        v_ref[rows, :] = v
    pprev[...] = p_last


def _pad_rows(w, rows, offset):
    out = jnp.zeros((rows, w.shape[1]), w.dtype)
    return out.at[offset:offset + w.shape[0]].set(w)


def _rwkv_pre(x, ln1, w_in_all, layer, mu_rkv, mu_x, w0, w1, w2, a0, a1, a2, g1, g2, vres,
              v_first):
    seq, d = x.shape
    d_mix = w0.shape[0]
    d_memq = w_in_all.shape[2] - 3 * d_mix
    lora_w, lora_a, lora_g = w1.shape[1], a1.shape[1], g1.shape[1]
    assert lora_w + lora_a == LANES and lora_g == LANES
    has_vres = vres is not None
    firsts = [w1, a1, g1]
    mus = [mu_x[0], mu_x[1], mu_x[2]]
    if has_vres:
        mu_v, v0, v1, v2 = vres
        assert v1.shape[1] <= LANES
        firsts.append(jnp.pad(v1, ((0, 0), (0, LANES - v1.shape[1]))))
        mus.append(mu_v)
    wl_h = jnp.concatenate(firsts, axis=1)
    wl_d = jnp.concatenate([m[:, None] * w for m, w in zip(mus, firsts)], axis=1)
    lp = wl_h.shape[1]
    w2p = _pad_rows(w2, LANES, 0)
    a2p = _pad_rows(a2, LANES, lora_w)

    tm = min(ROW_TILE, seq)
    assert seq % tm == 0
    row = lambda n: pl.BlockSpec((tm, n), lambda i: (i, 0))
    ins = [x, ln1.reshape(1, d), w_in_all, wl_h.astype(BF16), wl_d.astype(BF16),
           w2p.astype(BF16), a2p.astype(BF16), g2.astype(BF16), mu_rkv,
           w0.reshape(1, d_mix), a0.reshape(1, d_mix)]
    specs = [row(d), _const_spec((1, d)), _layer_spec(w_in_all.shape, layer),
             _const_spec((d, lp)),
             _const_spec((d, lp)), _const_spec((LANES, d_mix)), _const_spec((LANES, d_mix)),
             _const_spec((LANES, d_mix)), _const_spec(mu_rkv.shape),
             _const_spec((1, d_mix)), _const_spec((1, d_mix))]
    if has_vres:
        ins += [_pad_rows(v2, LANES, 0).astype(BF16), v0.reshape(1, d_mix), v_first]
        specs += [_const_spec((LANES, d_mix)), _const_spec((1, d_mix)), row(d_mix)]
    mix_out = jax.ShapeDtypeStruct((seq, d_mix), F32)
    return pl.pallas_call(
        functools.partial(_rwkv_pre_kernel, has_vres, d_mix),
        grid=(seq // tm,),
        in_specs=specs,
        out_specs=[row(d_mix)] * 6 + [row(d_memq)],
        out_shape=[mix_out] * 6 + [jax.ShapeDtypeStruct((seq, d_memq), BF16)],
        scratch_shapes=[pltpu.VMEM((1, d), F32), pltpu.VMEM((1, 3 * d_mix), F32)],
        compiler_params=_params("arbitrary"),
    )(*ins)


def _scan_chunks(chunks, gates, kk_w, ka_w, rk_w, lnw, lnb, state, consts):
    tri_incl_b, strict, incl, eye, same_head, head0 = consts
    n = SCAN_CHUNK
    heads = range(HEADS_PER_GROUP)
    head_mask = [head0, jnp.logical_not(head0)]
    nc = len(chunks)

    def head_sums(xs):
        out = []
        for x in xs:
            s0 = jnp.sum(jnp.where(head0, x, 0.0), axis=-1, keepdims=True)
            s1 = jnp.sum(jnp.where(head0, 0.0, x), axis=-1, keepdims=True)
            out.append(jnp.where(head0, s0, s1))
        return out

    kk = [k * kk_w for (_, _, k, _, _) in chunks]
    kmod = [k * (1.0 + (a - 1.0) * ka_w) for (_, _, k, _, a) in chunks]
    sums = head_sums([x * x for x in kk] + [chunks[j][0] * kmod[j] * rk_w for j in range(nc)])
    ss, bonus = sums[:nc], sums[nc:]
    tri2 = jnp.concatenate([tri_incl_b, tri_incl_b], axis=1)
    cum = [jnp.dot(tri2, jnp.concatenate(_split(lw), axis=0), preferred_element_type=F32)
           for (_, lw, _, _, _) in chunks]

    pre = []
    for j, (r, lw, k, v, a) in enumerate(chunks):
        kkn = kk[j] / jnp.maximum(jnp.sqrt(ss[j]), 1e-12)
        c = cum[j]
        ref = c[n // 2 - 1:n // 2, :]
        inv = jnp.exp(ref - c)
        rt = r * jnp.exp(c - ref)
        at = -kkn * jnp.exp(c - lw - ref)
        lhs = [jnp.where(m, x, 0.0).astype(BF16) for m in head_mask for x in (at, rt)]
        pre.append(dict(
            ref=ref, e_last=jnp.exp(c[n - 1:n, :] - ref), rt=rt, at=at.astype(BF16),
            vb=v.astype(BF16), lhs=jnp.concatenate(lhs, axis=0),
            bk=jnp.concatenate([(kkn * a * inv).astype(BF16), (kmod[j] * inv).astype(BF16)],
                               axis=0)))

    quad = [_bdot(pre[j]["lhs"], pre[j]["bk"], NT) for j in range(nc)]
    chains = [(j, h) for j in range(nc) for h in heads]
    strict2 = jnp.concatenate([strict, strict], axis=1)
    incl2 = jnp.concatenate([incl, incl], axis=1)
    ab_ak = {(j, h): jnp.where(strict2, quad[j][2 * h * n:(2 * h + 1) * n], 0.0)
             for j, h in chains}
    rb_rk = {(j, h): jnp.where(incl2, quad[j][(2 * h + 1) * n:(2 * h + 2) * n], 0.0).astype(BF16)
             for j, h in chains}

    nmat = {ch: ab_ak[ch][:, :n] for ch in chains}

    def lane_block(size):
        return lax.broadcasted_iota(jnp.int32, (size, n), 1) // size

    def pack_diag(mat, size, blk):
        out = mat[n - size:]
        for a in range(n // size - 2, -1, -1):
            out = jnp.where(blk == a, mat[a * size:(a + 1) * size], out)
        return out

    def block_diag(packed, size, blk):
        return jnp.concatenate([jnp.where(blk == a, packed, jnp.zeros_like(packed))
                                for a in range(n // size)], axis=0)

    size = INV_BASE
    blk = lane_block(size)
    eye_packed = pack_diag(eye, size, blk)
    packed = {ch: pack_diag(nmat[ch], size, blk) for ch in chains}
    t_p = {ch: eye_packed + packed[ch] for ch in chains}
    pb = {ch: packed[ch].astype(BF16) for ch in chains}
    power = {ch: _bdot(pb[ch], block_diag(pb[ch], size, blk)).astype(BF16) for ch in chains}
    for _ in range(int(math.log2(size)) - 2):
        both = {ch: _bdot(jnp.concatenate([t_p[ch].astype(BF16), power[ch]], axis=0),
                          block_diag(power[ch], size, blk)) for ch in chains}
        t_p = {ch: t_p[ch] + both[ch][:size] for ch in chains}
        power = {ch: both[ch][size:].astype(BF16) for ch in chains}
    t_p = {ch: t_p[ch] + _bdot(t_p[ch], block_diag(power[ch], size, blk)) for ch in chains}

    while size < n:
        odd = (blk % 2) == 1
        tp_b = {ch: t_p[ch].astype(BF16) for ch in chains}
        sub = {}
        for ch in chains:
            y = jnp.zeros((size, n), F32)
            for a in range(0, n // size, 2):
                y = jnp.where(blk == a, nmat[ch][(a + 1) * size:(a + 2) * size], y)
            sub[ch] = _bdot(y, block_diag(tp_b[ch], size, blk)).astype(BF16)
        zeros_blk = jnp.zeros((size, n), BF16)
        off = {ch: _bdot(jnp.where(odd, tp_b[ch], jnp.zeros_like(tp_b[ch])),
                         jnp.concatenate(
                             [zeros_blk if a % 2 == 0
                              else jnp.where(blk == a - 1, sub[ch], jnp.zeros_like(sub[ch]))
                              for a in range(n // size)], axis=0))
               for ch in chains}
        t_p = {ch: jnp.concatenate([jnp.where(odd, 0.0, t_p[ch]),
                                    off[ch] + jnp.where(odd, t_p[ch], 0.0)], axis=0)
               for ch in chains}
        size *= 2
        blk = lane_block(size)
    t_b = {ch: t_p[ch].astype(BF16) for ch in chains}

    def by_head(x):
        wide = (lax.broadcasted_iota(jnp.int32, x.shape, 1) % LANES) < HEAD_DIM
        return jnp.concatenate([jnp.where(wide, x, jnp.zeros_like(x)),
                                jnp.where(wide, jnp.zeros_like(x), x)], axis=0)

    def heads_on_k(mats, j):
        return jnp.concatenate([mats[(j, h)] for h in heads], axis=1)

    ak_b = {ch: ab_ak[ch][:, n:].astype(BF16) for ch in chains}
    akv = [_bdot(heads_on_k(ak_b, j), by_head(pre[j]["vb"])).astype(BF16) for j in range(nc)]
    t_x = [_bdot(heads_on_k(t_b, j), by_head(jnp.concatenate([pre[j]["at"], akv[j]], axis=1)))
           for j in range(nc)]
    ahat = [t[:, :LANES].astype(BF16) for t in t_x]
    uin = [t[:, LANES:].astype(BF16) for t in t_x]
    zero = jnp.zeros((n, LANES), BF16)
    r_x = [_bdot(heads_on_k(rb_rk, j),
                 by_head(jnp.concatenate([jnp.concatenate([ahat[j], uin[j]], axis=1),
                                          jnp.concatenate([zero, pre[j]["vb"]], axis=1)], axis=0)))
           for j in range(nc)]
    rhat = [pre[j]["rt"] + r_x[j][:, :LANES] for j in range(nc)]
    yin = [r_x[j][:, LANES:] for j in range(nc)]
    m_mat = [(eye + jnp.where(same_head, _bdot(ahat[j], pre[j]["bk"][:n], TN), 0.0))
             * pre[j]["e_last"] for j in range(nc)]
    c_mat = [jnp.where(same_head,
                       _bdot(jnp.concatenate([uin[j], pre[j]["vb"]], axis=0), pre[j]["bk"], TN),
                       0.0) * pre[j]["e_last"] for j in range(nc)]

    ys = []
    for j in range(nc):
        sp = state * jnp.exp(pre[j]["ref"])
        ys.append(_bdot(rhat[j], sp, NT) + yin[j])
        state = _bdot(sp, m_mat[j]) + c_mat[j]

    inv_hd = 1.0 / HEAD_DIM
    mean = [m * inv_hd for m in head_sums(ys)]
    dy = [y - m for y, m in zip(ys, mean)]
    var = [s * inv_hd for s in head_sums([d * d for d in dy])]
    outs = [(dy[j] * lax.rsqrt(var[j] + LN_X_EPS) * lnw + lnb + bonus[j] * chunks[j][3])
            * gates[j] for j in range(nc)]
    return outs, state


def _rwkv_scan_kernel(r_ref, lw_ref, k_ref, v_ref, a_ref, g_ref, kk_ref, ka_ref, rk_ref,
                      lnw_ref, lnb_ref, y_ref, state_ref):
    @pl.when(pl.program_id(1) == 0)
    def _():
        state_ref[...] = jnp.zeros_like(state_ref)

    n = SCAN_CHUNK
    row = lax.broadcasted_iota(jnp.int32, (n, n), 0)
    col = lax.broadcasted_iota(jnp.int32, (n, n), 1)
    incl = row >= col
    strict = row > col
    same_head = (row // HEAD_DIM) == (col // HEAD_DIM)
    consts = (incl.astype(BF16), strict, incl, (row == col).astype(F32), same_head,
              col < HEAD_DIM)

    nc = r_ref.shape[0] // n
    chunks = [tuple(ref[j * n:(j + 1) * n, :] for ref in (r_ref, lw_ref, k_ref, v_ref, a_ref))
              for j in range(nc)]
    gates = [g_ref[j * n:(j + 1) * n, :] for j in range(nc)]
    outs, state = _scan_chunks(chunks, gates, kk_ref[...], ka_ref[...], rk_ref[...],
                               lnw_ref[...], lnb_ref[...], state_ref[...], consts)
    for j in range(nc):
        y_ref[j * n:(j + 1) * n, :] = outs[j].astype(y_ref.dtype)
    state_ref[...] = state


def _rwkv_scan(r, lw, k, v, a, g, k_k, k_a, r_k, lnx_w, lnx_b):
    seq, d_mix = r.shape
    assert LANES == SCAN_CHUNK and d_mix % LANES == 0
    rows = min(SCAN_CHUNK * SCAN_CHUNKS_PER_STEP, seq)
    assert seq % rows == 0 and rows % SCAN_CHUNK == 0
    act = pl.BlockSpec((rows, LANES), lambda p, c: (c, p))
    par = pl.BlockSpec((1, LANES), lambda p, c: (0, p))
    flat = lambda t: t.reshape(1, d_mix)
    return pl.pallas_call(
        _rwkv_scan_kernel,
        grid=(d_mix // LANES, seq // rows),
        in_specs=[act] * 6 + [par] * 5,
        out_specs=act,
        out_shape=jax.ShapeDtypeStruct((seq, d_mix), BF16),
        scratch_shapes=[pltpu.VMEM((LANES, LANES), F32)],
        compiler_params=_params("parallel", "arbitrary"),
    )(r, lw, k, v, a, g, flat(k_k), flat(k_a), flat(r_k), flat(lnx_w), flat(lnx_b))


def _norm_proj_kernel(n_proj, *refs):
    x_ref = refs[0]
    gains = refs[1:1 + n_proj]
    weights = refs[1 + n_proj:1 + 2 * n_proj]
    outs = refs[1 + 2 * n_proj:]
    xn = _rms_scale(x_ref[...])
    o = 0
    for g_ref, w_ref in zip(gains, weights):
        y = _bdot(xn * g_ref[...], w_ref[...])
        start = 0
        while start < y.shape[1]:
            width = outs[o].shape[1]
            outs[o][...] = y[:, start:start + width].astype(outs[o].dtype)
            start += width
            o += 1


def _norm_proj(x, projections):
    seq, d = x.shape
    tm = min(ROW_TILE, seq)
    row = lambda n: pl.BlockSpec((tm, n), lambda i: (i, 0))
    gains = [g.reshape(1, d) for g, _, _ in projections]
    weights = [w.astype(BF16) for _, w, _ in projections]
    widths = [n for _, _, ws in projections for n in ws]
    return pl.pallas_call(
        functools.partial(_norm_proj_kernel, len(projections)),
        grid=(seq // tm,),
        in_specs=([row(d)] + [_const_spec((1, d))] * len(gains)
                  + [_const_spec(w.shape) for w in weights]),
        out_specs=[row(n) for n in widths],
        out_shape=[jax.ShapeDtypeStruct((seq, n), BF16) for n in widths],
        compiler_params=_params("arbitrary"),
    )(x, *gains, *weights)


def _band_bias(table):
    n_heads, n_tab = table.shape
    width = BAND + CHUNK - 1
    ext = jnp.concatenate(
        [table, jnp.broadcast_to(table[:, -1:], (n_heads, width - n_tab))], axis=1).astype(F32)
    rep = jnp.tile(ext, (1, CHUNK + 1))[:, :CHUNK * (width + 1)]
    windows = rep.reshape(n_heads, CHUNK, width + 1)[:, :, :BAND]
    return windows[:, :, ::-1]


def _chunk_attn_tile(first_tile, q_ref, kp_ref, kc_ref, vp_ref, vc_ref, bias_ref, o_ref):
    tq = q_ref.shape[0]
    k_win = jnp.concatenate([kp_ref[...], kc_ref[...]], axis=0)
    v_win = jnp.concatenate([vp_ref[...], vc_ref[...]], axis=0)
    bias = jnp.concatenate([bias_ref[hd] for hd in range(HEADS_PER_GROUP)], axis=0)
    lane = lax.broadcasted_iota(jnp.int32, (CHUNK, LANES), 1)
    col = lax.broadcasted_iota(jnp.int32, (HEADS_PER_GROUP * CHUNK, BAND), 1)
    scale = HEAD_DIM ** -0.5
    blocks = range(tq // CHUNK)
    scores = []
    for c in blocks:
        q = q_ref[c * CHUNK:(c + 1) * CHUNK, :] * scale
        q2 = jnp.concatenate(
            [jnp.where((lane // HEAD_DIM) == hd, q, jnp.zeros_like(q))
             for hd in range(HEADS_PER_GROUP)], axis=0)
        scores.append(_bdot(q2, k_win[c * CHUNK:c * CHUNK + BAND, :], NT))
    probs, denoms = [], []
    for c in blocks:
        s = scores[c] + bias
        if first_tile:
            s = jnp.where(col >= tq - c * CHUNK, s, MASK_VALUE)
        e = jnp.exp(s - jnp.max(s, axis=-1, keepdims=True))
        denoms.append(jnp.sum(e, axis=-1, keepdims=True))
        probs.append(e.astype(BF16))
    for c in blocks:
        o2 = _bdot(probs[c], v_win[c * CHUNK:c * CHUNK + BAND, :]) / denoms[c]
        out = o2[:CHUNK, :]
        for hd in range(1, HEADS_PER_GROUP):
            out = jnp.where((lane // HEAD_DIM) == hd, o2[hd * CHUNK:(hd + 1) * CHUNK, :], out)
        o_ref[c * CHUNK:(c + 1) * CHUNK, :] = out.astype(o_ref.dtype)


def _chunk_attn_kernel(*refs):
    is_first = pl.program_id(1) == 0
    pl.when(is_first)(functools.partial(_chunk_attn_tile, True, *refs))
    pl.when(jnp.logical_not(is_first))(functools.partial(_chunk_attn_tile, False, *refs))


def _chunk_attn(q, k, v, rel_table):
    seq, d_mix = q.shape
    tq = ATTN_TILE
    assert seq % tq == 0
    cur = pl.BlockSpec((tq, LANES), lambda p, i: (i, p))
    prev = pl.BlockSpec((tq, LANES), lambda p, i: (jnp.maximum(i - 1, 0), p))
    bias = _band_bias(rel_table)
    return pl.pallas_call(
        _chunk_attn_kernel,
        grid=(d_mix // LANES, seq // tq),
        in_specs=[cur, prev, cur, prev, cur,
                  pl.BlockSpec((HEADS_PER_GROUP, CHUNK, BAND), lambda p, i: (p, 0, 0))],
        out_specs=cur,
        out_shape=jax.ShapeDtypeStruct((seq, d_mix), BF16),
        compiler_params=_params("parallel", "arbitrary"),
    )(q, k, k, v, v, bias)


def _layer_post_kernel(has_final, d_mix, d_ff, *refs):
    refs = list(refs)
    (x_ref, mix_ref, qm_ref, kvm_ref, wout_ref, ln2_ref, fin_ref, cw_ref, cb_ref,
     fout_ref) = refs[:10]
    refs = refs[10:]
    lnf_ref = refs.pop(0) if has_final else None
    out_ref, carry = refs

    @pl.when(pl.program_id(0) == 0)
    def _():
        carry[...] = jnp.zeros_like(carry)

    tm = x_ref.shape[0]
    d_memq = qm_ref.shape[1]
    mix = mix_ref[...]

    qm = qm_ref[...] * HEAD_DIM ** -0.5
    km = kvm_ref[:, :d_memq]
    vm = kvm_ref[:, d_memq:]
    lane = lax.broadcasted_iota(jnp.int32, qm.shape, 1)
    head_masks = [(lane // HEAD_DIM) == hd for hd in range(d_memq // HEAD_DIM)]
    scores = [_bdot(jnp.where(mh, qm, jnp.zeros_like(qm)), km, NT) for mh in head_masks]
    proj_mix = _bdot(mix, wout_ref[:d_mix, :])
    exps = [jnp.exp(s - jnp.max(s, axis=-1, keepdims=True)) for s in scores]
    heads_out = [_bdot(e, vm) / jnp.sum(e, axis=-1, keepdims=True) for e in exps]
    mo = heads_out[0]
    for mh, o_h in zip(head_masks[1:], heads_out[1:]):
        mo = jnp.where(mh, o_h, mo)
    x1 = x_ref[...] + (proj_mix + _bdot(mo, wout_ref[d_mix:, :]))

    h2 = (_rms_scale(x1) * ln2_ref[...]).astype(BF16)

    def in_dots(c0):
        return (jnp.dot(h2, fin_ref[:, c0:c0 + FFN_COLS], preferred_element_type=F32),
                jnp.dot(h2, fin_ref[:, d_ff + c0:d_ff + c0 + FFN_COLS],
                        preferred_element_type=F32))

    slabs = list(range(0, d_ff, FFN_COLS))
    ffn = None
    nxt = in_dots(slabs[0])
    for idx, c0 in enumerate(slabs):
        cols = slice(c0, c0 + FFN_COLS)
        gate, val = nxt
        if idx + 1 < len(slabs):
            nxt = in_dots(slabs[idx + 1])
        prev = carry[SUBLANES - (CONV_W - 1):, cols]
        carry[:, cols] = gate[tm - SUBLANES:, :]
        conv = cb_ref[:, cols] + cw_ref[CONV_W - 1:CONV_W, cols] * gate
        for j in range(1, CONV_W):
            conv = conv + (cw_ref[CONV_W - 1 - j:CONV_W - j, cols]
                           * _shift_rows(gate, prev[CONV_W - 1 - j:, :], j))
        act = 0.5 * conv * (1.0 + lax.erf(conv * (2.0 ** -0.5))) * val
        slab = _bdot(act, fout_ref[cols, :])
        ffn = slab if ffn is None else ffn + slab
    acc = x1 + ffn
    if has_final:
        acc = _rms_scale(acc) * lnf_ref[...]
    out_ref[...] = acc


def _layer_post(x, mix, qm, layer, kv_mem, w_out, ln2, ffn_in, conv_w, conv_b, ffn_out, ln_f):
    seq, d = x.shape
    d_mix = mix.shape[1]
    d_memq = qm.shape[1]
    d_ff = ffn_out.shape[1]
    assert d_ff % FFN_COLS == 0 and conv_w.shape[0] == CONV_W
    tm = min(POST_TILE, seq)
    row = lambda n: pl.BlockSpec((tm, n), lambda i: (i, 0))
    ins, specs = [x, mix], [row(d), row(d_mix)]
    ins += [qm, kv_mem, w_out, ln2.reshape(1, d), ffn_in, conv_w, conv_b.reshape(1, d_ff),
            ffn_out]
    specs += [row(d_memq), _layer_spec(kv_mem.shape, layer), _layer_spec(w_out.shape, layer),
              _const_spec((1, d)), _layer_spec(ffn_in.shape, layer), _const_spec(conv_w.shape),
              _const_spec((1, d_ff)), _layer_spec(ffn_out.shape, layer)]
    if ln_f is not None:
        ins.append(ln_f.reshape(1, d))
        specs.append(_const_spec((1, d)))
    return pl.pallas_call(
        functools.partial(_layer_post_kernel, ln_f is not None, d_mix, d_ff),
        grid=(seq // tm,),
        in_specs=specs,
        out_specs=row(d),
        out_shape=jax.ShapeDtypeStruct((seq, d), F32),
        scratch_shapes=[pltpu.VMEM((SUBLANES, d_ff), F32)],
        compiler_params=_params("arbitrary"),
    )(*ins)


def kernel(x, mem, mem_norm, ln1, ln2, w_out, w_mem_kv, ffn_in, ffn_conv, ffn_conv_b, ffn_out, a_w_in, a_mu_rkv, a_mu_x, a_w0, a_w1, a_w2, a_a0, a_a1, a_a2, a_g1, a_g2, a_k_k, a_k_a, a_r_k, a_lnx_w, a_lnx_b, a_mu_v, a_v0, a_v1, a_v2, ln_kv, w_kv, b_w_in, b_rel, ln_f):
    bsz, seq, d = x.shape
    assert bsz == 1 and mem.shape[0] == 1
    depth = ln1.shape[0]
    n_a = a_w_in.shape[0]
    d_mix = a_w0.shape[1]
    xs = x.reshape(seq, d)
    kv_mem = _mem_kv(mem.reshape(mem.shape[1], d), mem_norm, w_mem_kv)
    a_w_in_b, w_out_b = a_w_in.astype(BF16), w_out.astype(BF16)
    ffn_in_b, ffn_out_b = ffn_in.astype(BF16), ffn_out.astype(BF16)

    v_first = None
    k_s = v_s = None
    for layer in range(depth):
        last = ln_f if layer == depth - 1 else None
        if layer < n_a:
            i = layer
            vres = None if i == 0 else (a_mu_v[i - 1], a_v0[i - 1], a_v1[i - 1], a_v2[i - 1])
            r, lw, k, v, a, g, qm = _rwkv_pre(
                xs, ln1[layer], a_w_in_b, i, a_mu_rkv[i], a_mu_x[i], a_w0[i], a_w1[i], a_w2[i],
                a_a0[i], a_a1[i], a_a2[i], a_g1[i], a_g2[i], vres, v_first)
            if i == 0:
                v_first = v
            mix = _rwkv_scan(r, lw, k, v, a, g, a_k_k[i], a_k_a[i], a_r_k[i],
                             a_lnx_w[i], a_lnx_b[i])
        else:
            j = layer - n_a
            d_memq = b_w_in.shape[2] - d_mix
            projections = [(ln1[layer], b_w_in[j], [d_mix, d_memq])]
            if j == 0:
                projections.append((ln_kv, w_kv, [d_mix, d_mix]))
            outs = _norm_proj(xs, projections)
            q, qm = outs[:2]
            if j == 0:
                k_s, v_s = outs[2:]
            mix = _chunk_attn(q, k_s, v_s, b_rel[j])
        xs = _layer_post(xs, mix, qm, layer, kv_mem, w_out_b, ln2[layer], ffn_in_b,
                         ffn_conv[layer], ffn_conv_b[layer], ffn_out_b, last)
    return xs.reshape(bsz, seq, d)
```

```python
import functools
import math

import jax
import jax.numpy as jnp
from jax import lax
from jax.experimental import pallas as pl
from jax.experimental.pallas import tpu as pltpu

F32 = jnp.float32
BF16 = jnp.bfloat16

HEAD_DIM = 64
LANES = 128
HEADS_PER_GROUP = LANES // HEAD_DIM
SUBLANES = 8
CHUNK = 64
LEFT_CHUNKS = 8
BAND = (LEFT_CHUNKS + 1) * CHUNK
REL_CLIP = 256
CONV_W = 3
LN_X_EPS = 64e-5
RMS_EPS = 1e-6
MASK_VALUE = -1e30

SCAN_CHUNK = 128
SCAN_CHUNKS_PER_STEP = 16
SCAN_PARALLEL_STAGES = 13
INV_BASE = 64
ROW_TILE = 512
POST_TILE = 1024
PRE_SUBTILES = 2
ATTN_TILE = LEFT_CHUNKS * CHUNK
FFN_COLS = 256
VMEM_LIMIT = 56 * 1024 * 1024

NT = (((1,), (1,)), ((), ()))
TN = (((0,), (0,)), ((), ()))
NN = (((1,), (0,)), ((), ()))


def _bdot(a, b, dims=NN):
    return lax.dot_general(a.astype(BF16), b.astype(BF16), dims,
                           preferred_element_type=F32)


def _split(x):
    hi = x.astype(BF16)
    lo = (x - hi.astype(F32)).astype(BF16)
    return hi, lo


def _rms_scale(x):
    return x * lax.rsqrt(jnp.mean(x * x, axis=-1, keepdims=True) + RMS_EPS)


def _shift_rows(t, prev_rows, n):
    rolled = pltpu.roll(t, n, 0)
    row = lax.broadcasted_iota(jnp.int32, t.shape, 0)
    out = rolled
    for j in range(n):
        out = jnp.where(row == j, prev_rows[j:j + 1, :], out)
    return out


def _const_spec(shape):
    nd = len(shape)
    return pl.BlockSpec(shape, lambda *_: (0,) * nd, pipeline_mode=pl.Buffered(1))


def _layer_spec(stacked_shape, layer):
    nd = len(stacked_shape) - 1
    return pl.BlockSpec((None,) + tuple(stacked_shape[1:]), lambda *_: (layer,) + (0,) * nd,
                        pipeline_mode=pl.Buffered(1))


def _params(*sem):
    return pltpu.CompilerParams(dimension_semantics=sem, vmem_limit_bytes=VMEM_LIMIT)


def _mem_kv_kernel(mem_ref, g_ref, w_ref, out_ref):
    mem_n = _rms_scale(mem_ref[...]) * g_ref[...]
    out_ref[...] = _bdot(mem_n, w_ref[...]).astype(out_ref.dtype)


def _mem_kv(mem, mem_norm, w_mem_kv):
    depth, d, n = w_mem_kv.shape
    n_mem = mem.shape[0]
    return pl.pallas_call(
        _mem_kv_kernel,
        grid=(depth,),
        in_specs=[pl.BlockSpec((n_mem, d), lambda l: (0, 0)),
                  pl.BlockSpec((1, d), lambda l: (0, 0)),
                  pl.BlockSpec((None, d, n), lambda l: (l, 0, 0))],
        out_specs=pl.BlockSpec((None, n_mem, n), lambda l: (l, 0, 0)),
        out_shape=jax.ShapeDtypeStruct((depth, n_mem, n), BF16),
        compiler_params=_params("arbitrary"),
    )(mem, mem_norm.reshape(1, d), w_mem_kv.astype(BF16))


def _rwkv_pre_kernel(has_vres, d_mix, *refs):
    if has_vres:
        (x_ref, ln_ref, win_ref, wlh_ref, wld_ref, w2_ref, a2_ref, g2_ref, mu_ref,
         w0_ref, a0_ref, v2_ref, v0_ref, vf_ref,
         r_ref, lw_ref, k_ref, v_ref, a_ref, g_ref, qm_ref, hprev, pprev) = refs
    else:
        (x_ref, ln_ref, win_ref, wlh_ref, wld_ref, w2_ref, a2_ref, g2_ref, mu_ref,
         w0_ref, a0_ref,
         r_ref, lw_ref, k_ref, v_ref, a_ref, g_ref, qm_ref, hprev, pprev) = refs

    @pl.when(pl.program_id(0) == 0)
    def _():
        hprev[...] = jnp.zeros_like(hprev)
        pprev[...] = jnp.zeros_like(pprev)

    sub = x_ref.shape[0] // PRE_SUBTILES
    tiles = [slice(s * sub, (s + 1) * sub) for s in range(PRE_SUBTILES)]
    h_last = hprev[...]
    first = []
    for rows in tiles:
        h = _rms_scale(x_ref[rows, :]) * ln_ref[...]
        dh = _shift_rows(h, h_last, 1) - h
        h_last = h[sub - 1:sub, :]
        hb = h.astype(BF16)
        p = jnp.dot(hb, win_ref[...], preferred_element_type=F32)
        l1 = (jnp.dot(hb, wlh_ref[...], preferred_element_type=F32)
              + jnp.dot(dh.astype(BF16), wld_ref[...], preferred_element_type=F32))
        first.append((p, l1))
    hprev[...] = h_last

    mu = mu_ref[...]
    p_last = pprev[...]
    for rows, (p, l1) in zip(tiles, first):
        prkv = p[:, :3 * d_mix]
        ps = _shift_rows(prkv, p_last, 1)
        p_last = prkv[sub - 1:sub, :]

        def lerp(j):
            cur = prkv[:, j * d_mix:(j + 1) * d_mix]
            return cur + (ps[:, j * d_mix:(j + 1) * d_mix] - cur) * mu[j:j + 1, :]

        r_ref[rows, :] = lerp(0)
        k_ref[rows, :] = lerp(1)
        v = lerp(2)
        qm_ref[rows, :] = p[:, 3 * d_mix:].astype(qm_ref.dtype)

        l_wa = l1[:, :LANES]
        z = w0_ref[...] + _bdot(jnp.tanh(l_wa), w2_ref[...])
        lw_ref[rows, :] = -math.exp(-0.5) * jax.nn.sigmoid(z)
        a_ref[rows, :] = jax.nn.sigmoid(a0_ref[...] + _bdot(l_wa, a2_ref[...]))
        g_ref[rows, :] = _bdot(jax.nn.sigmoid(l1[:, LANES:2 * LANES]), g2_ref[...])
        if has_vres:
            gate = jax.nn.sigmoid(v0_ref[...]
                                  + _bdot(l1[:, 2 * LANES:3 * LANES], v2_ref[...]))
            v = v + (vf_ref[rows, :] - v) * gate
        v_ref[rows, :] = v
    pprev[...] = p_last


def _pad_rows(w, rows, offset):
    out = jnp.zeros((rows, w.shape[1]), w.dtype)
    return out.at[offset:offset + w.shape[0]].set(w)


def _rwkv_pre(x, ln1, w_in_all, layer, mu_rkv, mu_x, w0, w1, w2, a0, a1, a2, g1, g2, vres,
              v_first):
    seq, d = x.shape
    d_mix = w0.shape[0]
    d_memq = w_in_all.shape[2] - 3 * d_mix
    lora_w, lora_a, lora_g = w1.shape[1], a1.shape[1], g1.shape[1]
    assert lora_w + lora_a == LANES and lora_g == LANES
    has_vres = vres is not None
    firsts = [w1, a1, g1]
    mus = [mu_x[0], mu_x[1], mu_x[2]]
    if has_vres:
        mu_v, v0, v1, v2 = vres
        assert v1.shape[1] <= LANES
        firsts.append(jnp.pad(v1, ((0, 0), (0, LANES - v1.shape[1]))))
        mus.append(mu_v)
    wl_h = jnp.concatenate(firsts, axis=1)
    wl_d = jnp.concatenate([m[:, None] * w for m, w in zip(mus, firsts)], axis=1)
    lp = wl_h.shape[1]
    w2p = _pad_rows(w2, LANES, 0)
    a2p = _pad_rows(a2, LANES, lora_w)

    tm = min(ROW_TILE, seq)
    assert seq % tm == 0
    row = lambda n: pl.BlockSpec((tm, n), lambda i: (i, 0))
    ins = [x, ln1.reshape(1, d), w_in_all, wl_h.astype(BF16), wl_d.astype(BF16),
           w2p.astype(BF16), a2p.astype(BF16), g2.astype(BF16), mu_rkv,
           w0.reshape(1, d_mix), a0.reshape(1, d_mix)]
    specs = [row(d), _const_spec((1, d)), _layer_spec(w_in_all.shape, layer),
             _const_spec((d, lp)),
             _const_spec((d, lp)), _const_spec((LANES, d_mix)), _const_spec((LANES, d_mix)),
             _const_spec((LANES, d_mix)), _const_spec(mu_rkv.shape),
             _const_spec((1, d_mix)), _const_spec((1, d_mix))]
    if has_vres:
        ins += [_pad_rows(v2, LANES, 0).astype(BF16), v0.reshape(1, d_mix), v_first]
        specs += [_const_spec((LANES, d_mix)), _const_spec((1, d_mix)), row(d_mix)]
    mix_out = jax.ShapeDtypeStruct((seq, d_mix), F32)
    return pl.pallas_call(
        functools.partial(_rwkv_pre_kernel, has_vres, d_mix),
        grid=(seq // tm,),
        in_specs=specs,
        out_specs=[row(d_mix)] * 6 + [row(d_memq)],
        out_shape=[mix_out] * 6 + [jax.ShapeDtypeStruct((seq, d_memq), BF16)],
        scratch_shapes=[pltpu.VMEM((1, d), F32), pltpu.VMEM((1, 3 * d_mix), F32)],
        compiler_params=_params("arbitrary"),
    )(*ins)


def _head_sums(xs, head0):
    out = []
    for x in xs:
        s0 = jnp.sum(jnp.where(head0, x, 0.0), axis=-1, keepdims=True)
        s1 = jnp.sum(jnp.where(head0, 0.0, x), axis=-1, keepdims=True)
        out.append(jnp.where(head0, s0, s1))
    return out


def _scan_parallel(chunks, gates, kk_w, ka_w, rk_w, consts):
    tri_incl_b, strict, incl, eye, same_head, head0 = consts
    n = SCAN_CHUNK
    heads = range(HEADS_PER_GROUP)
    head_mask = [head0, jnp.logical_not(head0)]
    nc = len(chunks)

    kk = [k * kk_w for (_, _, k, _, _) in chunks]
    kmod = [k * (1.0 + (a - 1.0) * ka_w) for (_, _, k, _, a) in chunks]
    sums = _head_sums([x * x for x in kk] + [chunks[j][0] * kmod[j] * rk_w for j in range(nc)],
                      head0)
    ss, bonus = sums[:nc], sums[nc:]
    tri2 = jnp.concatenate([tri_incl_b, tri_incl_b], axis=1)
    cum = [jnp.dot(tri2, jnp.concatenate(_split(lw), axis=0), preferred_element_type=F32)
           for (_, lw, _, _, _) in chunks]
    yield

    pre = []
    for j, (r, lw, k, v, a) in enumerate(chunks):
        kkn = kk[j] / jnp.maximum(jnp.sqrt(ss[j]), 1e-12)
        c = cum[j]
        ref = c[n // 2 - 1:n // 2, :]
        inv = jnp.exp(ref - c)
        rt = r * jnp.exp(c - ref)
        at = -kkn * jnp.exp(c - lw - ref)
        lhs = [jnp.where(m, x, 0.0).astype(BF16) for m in head_mask for x in (at, rt)]
        pre.append(dict(
            ref=ref, e_last=jnp.exp(c[n - 1:n, :] - ref), rt=rt, at=at.astype(BF16),
            vb=v.astype(BF16), lhs=jnp.concatenate(lhs, axis=0),
            bk=jnp.concatenate([(kkn * a * inv).astype(BF16), (kmod[j] * inv).astype(BF16)],
                               axis=0)))

    quad = [_bdot(pre[j]["lhs"], pre[j]["bk"], NT) for j in range(nc)]
    yield
    chains = [(j, h) for j in range(nc) for h in heads]
    strict2 = jnp.concatenate([strict, strict], axis=1)
    incl2 = jnp.concatenate([incl, incl], axis=1)
    ab_ak = {(j, h): jnp.where(strict2, quad[j][2 * h * n:(2 * h + 1) * n], 0.0)
             for j, h in chains}
    rb_rk = {(j, h): jnp.where(incl2, quad[j][(2 * h + 1) * n:(2 * h + 2) * n], 0.0).astype(BF16)
             for j, h in chains}

    nmat = {ch: ab_ak[ch][:, :n] for ch in chains}

    def lane_block(size):
        return lax.broadcasted_iota(jnp.int32, (size, n), 1) // size

    def pack_diag(mat, size, blk):
        out = mat[n - size:]
        for a in range(n // size - 2, -1, -1):
            out = jnp.where(blk == a, mat[a * size:(a + 1) * size], out)
        return out

    def block_diag(packed, size, blk):
        return jnp.concatenate([jnp.where(blk == a, packed, jnp.zeros_like(packed))
                                for a in range(n // size)], axis=0)

    size = INV_BASE
    blk = lane_block(size)
    eye_packed = pack_diag(eye, size, blk)
    packed = {ch: pack_diag(nmat[ch], size, blk) for ch in chains}
    t_p = {ch: eye_packed + packed[ch] for ch in chains}
    pb = {ch: packed[ch].astype(BF16) for ch in chains}
    power = {ch: _bdot(pb[ch], block_diag(pb[ch], size, blk)).astype(BF16) for ch in chains}
    yield
    for _ in range(int(math.log2(size)) - 2):
        both = {ch: _bdot(jnp.concatenate([t_p[ch].astype(BF16), power[ch]], axis=0),
                          block_diag(power[ch], size, blk)) for ch in chains}
        yield
        t_p = {ch: t_p[ch] + both[ch][:size] for ch in chains}
        power = {ch: both[ch][size:].astype(BF16) for ch in chains}
    t_p = {ch: t_p[ch] + _bdot(t_p[ch], block_diag(power[ch], size, blk)) for ch in chains}
    yield

    while size < n:
        odd = (blk % 2) == 1
        tp_b = {ch: t_p[ch].astype(BF16) for ch in chains}
        sub = {}
        for ch in chains:
            y = jnp.zeros((size, n), F32)
            for a in range(0, n // size, 2):
                y = jnp.where(blk == a, nmat[ch][(a + 1) * size:(a + 2) * size], y)
            sub[ch] = _bdot(y, block_diag(tp_b[ch], size, blk)).astype(BF16)
        yield
        zeros_blk = jnp.zeros((size, n), BF16)
        off = {ch: _bdot(jnp.where(odd, tp_b[ch], jnp.zeros_like(tp_b[ch])),
                         jnp.concatenate(
                             [zeros_blk if a % 2 == 0
                              else jnp.where(blk == a - 1, sub[ch], jnp.zeros_like(sub[ch]))
                              for a in range(n // size)], axis=0))
               for ch in chains}
        yield
        t_p = {ch: jnp.concatenate([jnp.where(odd, 0.0, t_p[ch]),
                                    off[ch] + jnp.where(odd, t_p[ch], 0.0)], axis=0)
               for ch in chains}
        size *= 2
        blk = lane_block(size)
    t_b = {ch: t_p[ch].astype(BF16) for ch in chains}

    def by_head(x):
        wide = (lax.broadcasted_iota(jnp.int32, x.shape, 1) % LANES) < HEAD_DIM
        return jnp.concatenate([jnp.where(wide, x, jnp.zeros_like(x)),
                                jnp.where(wide, jnp.zeros_like(x), x)], axis=0)

    def heads_on_k(mats, j):
        return jnp.concatenate([mats[(j, h)] for h in heads], axis=1)

    ak_b = {ch: ab_ak[ch][:, n:].astype(BF16) for ch in chains}
    akv = [_bdot(heads_on_k(ak_b, j), by_head(pre[j]["vb"])).astype(BF16) for j in range(nc)]
    yield
    t_x = [_bdot(heads_on_k(t_b, j), by_head(jnp.concatenate([pre[j]["at"], akv[j]], axis=1)))
           for j in range(nc)]
    yield
    ahat = [t[:, :LANES].astype(BF16) for t in t_x]
    uin = [t[:, LANES:].astype(BF16) for t in t_x]
    zero = jnp.zeros((n, LANES), BF16)
    r_x = [_bdot(heads_on_k(rb_rk, j),
                 by_head(jnp.concatenate([jnp.concatenate([ahat[j], uin[j]], axis=1),
                                          jnp.concatenate([zero, pre[j]["vb"]], axis=1)], axis=0)))
           for j in range(nc)]
    yield
    rhat = [pre[j]["rt"] + r_x[j][:, :LANES] for j in range(nc)]
    yin = [r_x[j][:, LANES:] for j in range(nc)]
    m_mat = [(eye + jnp.where(same_head, _bdot(ahat[j], pre[j]["bk"][:n], TN), 0.0))
             * pre[j]["e_last"] for j in range(nc)]
    c_mat = [jnp.where(same_head,
                       _bdot(jnp.concatenate([uin[j], pre[j]["vb"]], axis=0), pre[j]["bk"], TN),
                       0.0) * pre[j]["e_last"] for j in range(nc)]

    yield
    return [dict(rhat=rhat[j].astype(BF16), yin=yin[j], m_mat=m_mat[j].astype(BF16),
                 c_mat=c_mat[j], eref=jnp.broadcast_to(jnp.exp(pre[j]["ref"]), (SUBLANES, LANES)),
                 gate=gates[j], bvg=bonus[j] * chunks[j][3] * gates[j]) for j in range(nc)]


SCAN_SLOT_FIELDS = ("rhat", "yin", "m_mat", "c_mat", "eref", "gate", "bvg")


def _scan_sequential(slots, y_ref, state_ref, lnw, lnb, head0):
    n = SCAN_CHUNK
    rhat_s, yin_s, m_s, c_s, eref_s, gate_s, bvg_s = slots
    state = state_ref[...]
    inv_hd = 1.0 / HEAD_DIM
    for j in range(rhat_s.shape[0]):
        sp = state * eref_s[j, 0:1, :]
        y = _bdot(rhat_s[j], sp, NT) + yin_s[j]
        state = _bdot(sp, m_s[j]) + c_s[j]
        mean = _head_sums([y], head0)[0] * inv_hd
        dy = y - mean
        var = _head_sums([dy * dy], head0)[0] * inv_hd
        out = (dy * lax.rsqrt(var + LN_X_EPS) * lnw + lnb) * gate_s[j] + bvg_s[j]
        y_ref[j * n:(j + 1) * n, :] = out.astype(y_ref.dtype)
        yield
    state_ref[...] = state


def _interleave(par, seq, n_seq):
    stages = done = 0
    while True:
        try:
            next(par)
        except StopIteration as stop:
            result = stop.value
            break
        stages += 1
        while done < min(n_seq, -(-stages * n_seq // SCAN_PARALLEL_STAGES)):
            next(seq)
            done += 1
    for _ in seq:
        pass
    return result


def _rwkv_scan_kernel(r_ref, lw_ref, k_ref, v_ref, a_ref, g_ref, kk_ref, ka_ref, rk_ref,
                      lnw_ref, lnb_ref, y_ref, state_ref, *slots):
    step = pl.program_id(1)
    n_blocks = pl.num_programs(1) - 1

    @pl.when(step == 0)
    def _():
        state_ref[...] = jnp.zeros_like(state_ref)
        for slot in slots:
            slot[...] = jnp.zeros_like(slot)

    n = SCAN_CHUNK
    row = lax.broadcasted_iota(jnp.int32, (n, n), 0)
    col = lax.broadcasted_iota(jnp.int32, (n, n), 1)
    incl = row >= col
    strict = row > col
    same_head = (row // HEAD_DIM) == (col // HEAD_DIM)
    head0 = col < HEAD_DIM
    consts = (incl.astype(BF16), strict, incl, (row == col).astype(F32), same_head, head0)
    nc = r_ref.shape[0] // n

    def sequential():
        return _scan_sequential(slots, y_ref, state_ref, lnw_ref[...], lnb_ref[...], head0)

    @pl.when(step < n_blocks)
    def _():
        chunks = [tuple(ref[j * n:(j + 1) * n, :]
                        for ref in (r_ref, lw_ref, k_ref, v_ref, a_ref)) for j in range(nc)]
        gates = [g_ref[j * n:(j + 1) * n, :] for j in range(nc)]
        par = _scan_parallel(chunks, gates, kk_ref[...], ka_ref[...], rk_ref[...], consts)
        results = _interleave(par, sequential(), nc)
        for j, res in enumerate(results):
            for slot, field in zip(slots, SCAN_SLOT_FIELDS):
                slot[j] = res[field]

    @pl.when(step == n_blocks)
    def _():
        for _ in sequential():
            pass


def _rwkv_scan(r, lw, k, v, a, g, k_k, k_a, r_k, lnx_w, lnx_b):
    seq, d_mix = r.shape
    assert LANES == SCAN_CHUNK and d_mix % LANES == 0
    nc = min(SCAN_CHUNKS_PER_STEP, seq // SCAN_CHUNK)
    rows = SCAN_CHUNK * nc
    assert seq % rows == 0
    n_blocks = seq // rows
    act = pl.BlockSpec((rows, LANES), lambda p, c: (jnp.minimum(c, n_blocks - 1), p))
    out = pl.BlockSpec((rows, LANES), lambda p, c: (jnp.maximum(c - 1, 0), p))
    par = pl.BlockSpec((1, LANES), lambda p, c: (0, p))
    flat = lambda t: t.reshape(1, d_mix)
    tile = (nc, SCAN_CHUNK, LANES)
    slot_types = dict(rhat=(tile, BF16), yin=(tile, F32), m_mat=(tile, BF16), c_mat=(tile, F32),
                      eref=((nc, SUBLANES, LANES), F32), gate=(tile, F32), bvg=(tile, F32))
    return pl.pallas_call(
        _rwkv_scan_kernel,
        grid=(d_mix // LANES, n_blocks + 1),
        in_specs=[act] * 6 + [par] * 5,
        out_specs=out,
        out_shape=jax.ShapeDtypeStruct((seq, d_mix), BF16),
        scratch_shapes=([pltpu.VMEM((LANES, LANES), F32)]
                        + [pltpu.VMEM(*slot_types[f]) for f in SCAN_SLOT_FIELDS]),
        compiler_params=_params("parallel", "arbitrary"),
    )(r, lw, k, v, a, g, flat(k_k), flat(k_a), flat(r_k), flat(lnx_w), flat(lnx_b))


def _norm_proj_kernel(n_proj, *refs):
    x_ref = refs[0]
    gains = refs[1:1 + n_proj]
    weights = refs[1 + n_proj:1 + 2 * n_proj]
    outs = refs[1 + 2 * n_proj:]
    xn = _rms_scale(x_ref[...])
    o = 0
    for g_ref, w_ref in zip(gains, weights):
        y = _bdot(xn * g_ref[...], w_ref[...])
        start = 0
        while start < y.shape[1]:
            width = outs[o].shape[1]
            outs[o][...] = y[:, start:start + width].astype(outs[o].dtype)
            start += width
            o += 1


def _norm_proj(x, projections):
    seq, d = x.shape
    tm = min(ROW_TILE, seq)
    row = lambda n: pl.BlockSpec((tm, n), lambda i: (i, 0))
    gains = [g.reshape(1, d) for g, _, _ in projections]
    weights = [w.astype(BF16) for _, w, _ in projections]
    widths = [n for _, _, ws in projections for n in ws]
    return pl.pallas_call(
        functools.partial(_norm_proj_kernel, len(projections)),
        grid=(seq // tm,),
        in_specs=([row(d)] + [_const_spec((1, d))] * len(gains)
                  + [_const_spec(w.shape) for w in weights]),
        out_specs=[row(n) for n in widths],
        out_shape=[jax.ShapeDtypeStruct((seq, n), BF16) for n in widths],
        compiler_params=_params("arbitrary"),
    )(x, *gains, *weights)


def _band_bias(table):
    n_heads, n_tab = table.shape
    width = BAND + CHUNK - 1
    ext = jnp.concatenate(
        [table, jnp.broadcast_to(table[:, -1:], (n_heads, width - n_tab))], axis=1).astype(F32)
    rep = jnp.tile(ext, (1, CHUNK + 1))[:, :CHUNK * (width + 1)]
    windows = rep.reshape(n_heads, CHUNK, width + 1)[:, :, :BAND]
    return windows[:, :, ::-1]


def _chunk_attn_tile(first_tile, q_ref, kp_ref, kc_ref, vp_ref, vc_ref, bias_ref, o_ref):
    tq = q_ref.shape[0]
    k_win = jnp.concatenate([kp_ref[...], kc_ref[...]], axis=0)
    v_win = jnp.concatenate([vp_ref[...], vc_ref[...]], axis=0)
    bias = jnp.concatenate([bias_ref[hd] for hd in range(HEADS_PER_GROUP)], axis=0)
    lane = lax.broadcasted_iota(jnp.int32, (CHUNK, LANES), 1)
    col = lax.broadcasted_iota(jnp.int32, (HEADS_PER_GROUP * CHUNK, BAND), 1)
    scale = HEAD_DIM ** -0.5
    blocks = range(tq // CHUNK)
    scores = []
    for c in blocks:
        q = q_ref[c * CHUNK:(c + 1) * CHUNK, :] * scale
        q2 = jnp.concatenate(
            [jnp.where((lane // HEAD_DIM) == hd, q, jnp.zeros_like(q))
             for hd in range(HEADS_PER_GROUP)], axis=0)
        scores.append(_bdot(q2, k_win[c * CHUNK:c * CHUNK + BAND, :], NT))
    probs = []
    for c in blocks:
        s = scores[c] + bias
        if first_tile:
            s = jnp.where(col >= tq - c * CHUNK, s, MASK_VALUE)
        probs.append(jnp.exp(s - jnp.max(s, axis=-1, keepdims=True)).astype(BF16))
    ones = jnp.ones((BAND, LANES), BF16)
    for c in blocks:
        both = _bdot(probs[c], jnp.concatenate([v_win[c * CHUNK:c * CHUNK + BAND, :], ones],
                                               axis=1))
        o2 = both[:, :LANES] / both[:, LANES:]
        out = o2[:CHUNK, :]
        for hd in range(1, HEADS_PER_GROUP):
            out = jnp.where((lane // HEAD_DIM) == hd, o2[hd * CHUNK:(hd + 1) * CHUNK, :], out)
        o_ref[c * CHUNK:(c + 1) * CHUNK, :] = out.astype(o_ref.dtype)


def _chunk_attn_kernel(*refs):
    is_first = pl.program_id(1) == 0
    pl.when(is_first)(functools.partial(_chunk_attn_tile, True, *refs))
    pl.when(jnp.logical_not(is_first))(functools.partial(_chunk_attn_tile, False, *refs))


def _chunk_attn(q, k, v, rel_table):
    seq, d_mix = q.shape
    tq = ATTN_TILE
    assert seq % tq == 0
    cur = pl.BlockSpec((tq, LANES), lambda p, i: (i, p))
    prev = pl.BlockSpec((tq, LANES), lambda p, i: (jnp.maximum(i - 1, 0), p))
    bias = _band_bias(rel_table)
    return pl.pallas_call(
        _chunk_attn_kernel,
        grid=(d_mix // LANES, seq // tq),
        in_specs=[cur, prev, cur, prev, cur,
                  pl.BlockSpec((HEADS_PER_GROUP, CHUNK, BAND), lambda p, i: (p, 0, 0))],
        out_specs=cur,
        out_shape=jax.ShapeDtypeStruct((seq, d_mix), BF16),
        compiler_params=_params("parallel", "arbitrary"),
    )(q, k, k, v, v, bias)


def _layer_post_kernel(has_final, d_mix, d_ff, *refs):
    refs = list(refs)
    (x_ref, mix_ref, qm_ref, kvm_ref, wout_ref, ln2_ref, fin_ref, cw_ref, cb_ref,
     fout_ref) = refs[:10]
    refs = refs[10:]
    lnf_ref = refs.pop(0) if has_final else None
    out_ref, carry = refs

    @pl.when(pl.program_id(0) == 0)
    def _():
        carry[...] = jnp.zeros_like(carry)

    tm = x_ref.shape[0]
    d_memq = qm_ref.shape[1]
    mix = mix_ref[...]

    qm = qm_ref[...] * HEAD_DIM ** -0.5
    km = kvm_ref[:, :d_memq]
    vm = kvm_ref[:, d_memq:]
    lane = lax.broadcasted_iota(jnp.int32, qm.shape, 1)
    head_masks = [(lane // HEAD_DIM) == hd for hd in range(d_memq // HEAD_DIM)]
    scores = [_bdot(jnp.where(mh, qm, jnp.zeros_like(qm)), km, NT) for mh in head_masks]
    proj_mix = _bdot(mix, wout_ref[:d_mix, :])
    exps = [jnp.exp(s - jnp.max(s, axis=-1, keepdims=True)) for s in scores]
    heads_out = [_bdot(e, vm) / jnp.sum(e, axis=-1, keepdims=True) for e in exps]
    mo = heads_out[0]
    for mh, o_h in zip(head_masks[1:], heads_out[1:]):
        mo = jnp.where(mh, o_h, mo)
    x1 = x_ref[...] + (proj_mix + _bdot(mo, wout_ref[d_mix:, :]))

    h2 = (_rms_scale(x1) * ln2_ref[...]).astype(BF16)

    def in_dots(c0):
        return (jnp.dot(h2, fin_ref[:, c0:c0 + FFN_COLS], preferred_element_type=F32),
                jnp.dot(h2, fin_ref[:, d_ff + c0:d_ff + c0 + FFN_COLS],
                        preferred_element_type=F32))

    slabs = list(range(0, d_ff, FFN_COLS))
    ffn = None
    nxt = in_dots(slabs[0])
    for idx, c0 in enumerate(slabs):
        cols = slice(c0, c0 + FFN_COLS)
        gate, val = nxt
        if idx + 1 < len(slabs):
            nxt = in_dots(slabs[idx + 1])
        prev = carry[SUBLANES - (CONV_W - 1):, cols]
        carry[:, cols] = gate[tm - SUBLANES:, :]
        conv = cb_ref[:, cols] + cw_ref[CONV_W - 1:CONV_W, cols] * gate
        for j in range(1, CONV_W):
            conv = conv + (cw_ref[CONV_W - 1 - j:CONV_W - j, cols]
                           * _shift_rows(gate, prev[CONV_W - 1 - j:, :], j))
        act = 0.5 * conv * (1.0 + lax.erf(conv * (2.0 ** -0.5))) * val
        slab = _bdot(act, fout_ref[cols, :])
        ffn = slab if ffn is None else ffn + slab
    acc = x1 + ffn
    if has_final:
        acc = _rms_scale(acc) * lnf_ref[...]
    out_ref[...] = acc


def _layer_post(x, mix, qm, layer, kv_mem, w_out, ln2, ffn_in, conv_w, conv_b, ffn_out, ln_f):
    seq, d = x.shape
    d_mix = mix.shape[1]
    d_memq = qm.shape[1]
    d_ff = ffn_out.shape[1]
    assert d_ff % FFN_COLS == 0 and conv_w.shape[0] == CONV_W
    tm = min(POST_TILE, seq)
    row = lambda n: pl.BlockSpec((tm, n), lambda i: (i, 0))
    ins, specs = [x, mix], [row(d), row(d_mix)]
    ins += [qm, kv_mem, w_out, ln2.reshape(1, d), ffn_in, conv_w, conv_b.reshape(1, d_ff),
            ffn_out]
    specs += [row(d_memq), _layer_spec(kv_mem.shape, layer), _layer_spec(w_out.shape, layer),
              _const_spec((1, d)), _layer_spec(ffn_in.shape, layer), _const_spec(conv_w.shape),
              _const_spec((1, d_ff)), _layer_spec(ffn_out.shape, layer)]
    if ln_f is not None:
        ins.append(ln_f.reshape(1, d))
        specs.append(_const_spec((1, d)))
    return pl.pallas_call(
        functools.partial(_layer_post_kernel, ln_f is not None, d_mix, d_ff),
        grid=(seq // tm,),
        in_specs=specs,
        out_specs=row(d),
        out_shape=jax.ShapeDtypeStruct((seq, d), F32),
        scratch_shapes=[pltpu.VMEM((SUBLANES, d_ff), F32)],
        compiler_params=_params("arbitrary"),
    )(*ins)


def kernel(x, mem, mem_norm, ln1, ln2, w_out, w_mem_kv, ffn_in, ffn_conv, ffn_conv_b, ffn_out, a_w_in, a_mu_rkv, a_mu_x, a_w0, a_w1, a_w2, a_a0, a_a1, a_a2, a_g1, a_g2, a_k_k, a_k_a, a_r_k, a_lnx_w, a_lnx_b, a_mu_v, a_v0, a_v1, a_v2, ln_kv, w_kv, b_w_in, b_rel, ln_f):
    bsz, seq, d = x.shape
    assert bsz == 1 and mem.shape[0] == 1
    depth = ln1.shape[0]
    n_a = a_w_in.shape[0]
    d_mix = a_w0.shape[1]
    xs = x.reshape(seq, d)
    kv_mem = _mem_kv(mem.reshape(mem.shape[1], d), mem_norm, w_mem_kv)
    a_w_in_b, w_out_b = a_w_in.astype(BF16), w_out.astype(BF16)
    ffn_in_b, ffn_out_b = ffn_in.astype(BF16), ffn_out.astype(BF16)

    v_first = None
    k_s = v_s = None
    for layer in range(depth):
        last = ln_f if layer == depth - 1 else None
        if layer < n_a:
            i = layer
            vres = None if i == 0 else (a_mu_v[i - 1], a_v0[i - 1], a_v1[i - 1], a_v2[i - 1])
            r, lw, k, v, a, g, qm = _rwkv_pre(
                xs, ln1[layer], a_w_in_b, i, a_mu_rkv[i], a_mu_x[i], a_w0[i], a_w1[i], a_w2[i],
                a_a0[i], a_a1[i], a_a2[i], a_g1[i], a_g2[i], vres, v_first)
            if i == 0:
                v_first = v
            mix = _rwkv_scan(r, lw, k, v, a, g, a_k_k[i], a_k_a[i], a_r_k[i],
                             a_lnx_w[i], a_lnx_b[i])
        else:
            j = layer - n_a
            d_memq = b_w_in.shape[2] - d_mix
            projections = [(ln1[layer], b_w_in[j], [d_mix, d_memq])]
            if j == 0:
                projections.append((ln_kv, w_kv, [d_mix, d_mix]))
            outs = _norm_proj(xs, projections)
            q, qm = outs[:2]
            if j == 0:
                k_s, v_s = outs[2:]
            mix = _chunk_attn(q, k_s, v_s, b_rel[j])
        xs = _layer_post(xs, mix, qm, layer, kv_mem, w_out_b, ln2[layer], ffn_in_b,
                         ffn_conv[layer], ffn_conv_b[layer], ffn_out_b, last)
    return xs.reshape(bsz, seq, d)
```

```python
import functools
import math

import jax
import jax.numpy as jnp
from jax import lax
from jax.experimental import pallas as pl
from jax.experimental.pallas import tpu as pltpu

F32 = jnp.float32
BF16 = jnp.bfloat16

HEAD_DIM = 64
LANES = 128
HEADS_PER_GROUP = LANES // HEAD_DIM
SUBLANES = 8
CHUNK = 64
LEFT_CHUNKS = 8
BAND = (LEFT_CHUNKS + 1) * CHUNK
REL_CLIP = 256
CONV_W = 3
LN_X_EPS = 64e-5
RMS_EPS = 1e-6
MASK_VALUE = -1e30
LOG2_E = math.log2(math.e)
ATTN_Q_SCALE = HEAD_DIM ** -0.5 * LOG2_E

SCAN_CHUNK = 128
SCAN_CHUNKS_PER_STEP = 16
SCAN_PARALLEL_STAGES = 13
INV_BASE = 64
ROW_TILE = 512
POST_TILE = 1024
PRE_SUBTILES = 2
ATTN_TILE = LEFT_CHUNKS * CHUNK
FFN_COLS = 256
VMEM_LIMIT = 56 * 1024 * 1024

NT = (((1,), (1,)), ((), ()))
TN = (((0,), (0,)), ((), ()))
NN = (((1,), (0,)), ((), ()))


def _bdot(a, b, dims=NN):
    return lax.dot_general(a.astype(BF16), b.astype(BF16), dims,
                           preferred_element_type=F32)


def _split(x):
    hi = x.astype(BF16)
    lo = (x - hi.astype(F32)).astype(BF16)
    return hi, lo


def _rms_scale(x):
    return x * lax.rsqrt(jnp.mean(x * x, axis=-1, keepdims=True) + RMS_EPS)


def _shift_rows(t, prev_rows, n):
    rolled = pltpu.roll(t, n, 0)
    row = lax.broadcasted_iota(jnp.int32, t.shape, 0)
    out = rolled
    for j in range(n):
        out = jnp.where(row == j, prev_rows[j:j + 1, :], out)
    return out


def _const_spec(shape):
    nd = len(shape)
    return pl.BlockSpec(shape, lambda *_: (0,) * nd, pipeline_mode=pl.Buffered(1))


def _layer_spec(stacked_shape, layer):
    nd = len(stacked_shape) - 1
    return pl.BlockSpec((None,) + tuple(stacked_shape[1:]), lambda *_: (layer,) + (0,) * nd,
                        pipeline_mode=pl.Buffered(1))


def _params(*sem):
    return pltpu.CompilerParams(dimension_semantics=sem, vmem_limit_bytes=VMEM_LIMIT)


def _mem_kv_kernel(mem_ref, g_ref, w_ref, out_ref):
    mem_n = _rms_scale(mem_ref[...]) * g_ref[...]
    out_ref[...] = _bdot(mem_n, w_ref[...]).astype(out_ref.dtype)


def _mem_kv(mem, mem_norm, w_mem_kv):
    depth, d, n = w_mem_kv.shape
    n_mem = mem.shape[0]
    return pl.pallas_call(
        _mem_kv_kernel,
        grid=(depth,),
        in_specs=[pl.BlockSpec((n_mem, d), lambda l: (0, 0)),
                  pl.BlockSpec((1, d), lambda l: (0, 0)),
                  pl.BlockSpec((None, d, n), lambda l: (l, 0, 0))],
        out_specs=pl.BlockSpec((None, n_mem, n), lambda l: (l, 0, 0)),
        out_shape=jax.ShapeDtypeStruct((depth, n_mem, n), BF16),
        compiler_params=_params("arbitrary"),
    )(mem, mem_norm.reshape(1, d), w_mem_kv.astype(BF16))


def _rwkv_pre_kernel(has_vres, d_mix, *refs):
    if has_vres:
        (x_ref, ln_ref, win_ref, wlh_ref, wld_ref, w2_ref, a2_ref, g2_ref, mu_ref,
         w0_ref, a0_ref, v2_ref, v0_ref, vf_ref,
         r_ref, lw_ref, k_ref, v_ref, a_ref, g_ref, qm_ref, hprev, pprev) = refs
    else:
        (x_ref, ln_ref, win_ref, wlh_ref, wld_ref, w2_ref, a2_ref, g2_ref, mu_ref,
         w0_ref, a0_ref,
         r_ref, lw_ref, k_ref, v_ref, a_ref, g_ref, qm_ref, hprev, pprev) = refs

    @pl.when(pl.program_id(0) == 0)
    def _():
        hprev[...] = jnp.zeros_like(hprev)
        pprev[...] = jnp.zeros_like(pprev)

    sub = x_ref.shape[0] // PRE_SUBTILES
    tiles = [slice(s * sub, (s + 1) * sub) for s in range(PRE_SUBTILES)]
    h_last = hprev[...]
    first = []
    for rows in tiles:
        h = _rms_scale(x_ref[rows, :]) * ln_ref[...]
        dh = _shift_rows(h, h_last, 1) - h
        h_last = h[sub - 1:sub, :]
        hb = h.astype(BF16)
        p = jnp.dot(hb, win_ref[...], preferred_element_type=F32)
        l1 = (jnp.dot(hb, wlh_ref[...], preferred_element_type=F32)
              + jnp.dot(dh.astype(BF16), wld_ref[...], preferred_element_type=F32))
        first.append((p, l1))
    hprev[...] = h_last

    mu = mu_ref[...]
    p_last = pprev[...]
    for rows, (p, l1) in zip(tiles, first):
        prkv = p[:, :3 * d_mix]
        ps = _shift_rows(prkv, p_last, 1)
        p_last = prkv[sub - 1:sub, :]

        def lerp(j):
            cur = prkv[:, j * d_mix:(j + 1) * d_mix]
            return cur + (ps[:, j * d_mix:(j + 1) * d_mix] - cur) * mu[j:j + 1, :]

        r_ref[rows, :] = lerp(0)
        k_ref[rows, :] = lerp(1)
        v = lerp(2)
        qm_ref[rows, :] = p[:, 3 * d_mix:].astype(qm_ref.dtype)

        l_wa = l1[:, :LANES]
        z = w0_ref[...] + _bdot(jnp.tanh(l_wa), w2_ref[...])
        lw_ref[rows, :] = -math.exp(-0.5) * jax.nn.sigmoid(z)
        a_ref[rows, :] = jax.nn.sigmoid(a0_ref[...] + _bdot(l_wa, a2_ref[...]))
        g_ref[rows, :] = _bdot(jax.nn.sigmoid(l1[:, LANES:2 * LANES]), g2_ref[...])
        if has_vres:
            gate = jax.nn.sigmoid(v0_ref[...]
                                  + _bdot(l1[:, 2 * LANES:3 * LANES], v2_ref[...]))
            v = v + (vf_ref[rows, :] - v) * gate
        v_ref[rows, :] = v
    pprev[...] = p_last


def _pad_rows(w, rows, offset):
    out = jnp.zeros((rows, w.shape[1]), w.dtype)
    return out.at[offset:offset + w.shape[0]].set(w)


def _rwkv_pre(x, ln1, w_in_all, layer, mu_rkv, mu_x, w0, w1, w2, a0, a1, a2, g1, g2, vres,
              v_first):
    seq, d = x.shape
    d_mix = w0.shape[0]
    d_memq = w_in_all.shape[2] - 3 * d_mix
    lora_w, lora_a, lora_g = w1.shape[1], a1.shape[1], g1.shape[1]
    assert lora_w + lora_a == LANES and lora_g == LANES
    has_vres = vres is not None
    firsts = [w1, a1, g1]
    mus = [mu_x[0], mu_x[1], mu_x[2]]
    if has_vres:
        mu_v, v0, v1, v2 = vres
        assert v1.shape[1] <= LANES
        firsts.append(jnp.pad(v1, ((0, 0), (0, LANES - v1.shape[1]))))
        mus.append(mu_v)
    wl_h = jnp.concatenate(firsts, axis=1)
    wl_d = jnp.concatenate([m[:, None] * w for m, w in zip(mus, firsts)], axis=1)
    lp = wl_h.shape[1]
    w2p = _pad_rows(w2, LANES, 0)
    a2p = _pad_rows(a2, LANES, lora_w)

    tm = min(ROW_TILE, seq)
    assert seq % tm == 0
    row = lambda n: pl.BlockSpec((tm, n), lambda i: (i, 0))
    ins = [x, ln1.reshape(1, d), w_in_all, wl_h.astype(BF16), wl_d.astype(BF16),
           w2p.astype(BF16), a2p.astype(BF16), g2.astype(BF16), mu_rkv,
           w0.reshape(1, d_mix), a0.reshape(1, d_mix)]
    specs = [row(d), _const_spec((1, d)), _layer_spec(w_in_all.shape, layer),
             _const_spec((d, lp)),
             _const_spec((d, lp)), _const_spec((LANES, d_mix)), _const_spec((LANES, d_mix)),
             _const_spec((LANES, d_mix)), _const_spec(mu_rkv.shape),
             _const_spec((1, d_mix)), _const_spec((1, d_mix))]
    if has_vres:
        ins += [_pad_rows(v2, LANES, 0).astype(BF16), v0.reshape(1, d_mix), v_first]
        specs += [_const_spec((LANES, d_mix)), _const_spec((1, d_mix)), row(d_mix)]
    mix_out = jax.ShapeDtypeStruct((seq, d_mix), F32)
    return pl.pallas_call(
        functools.partial(_rwkv_pre_kernel, has_vres, d_mix),
        grid=(seq // tm,),
        in_specs=specs,
        out_specs=[row(d_mix)] * 6 + [row(d_memq)],
        out_shape=[mix_out] * 6 + [jax.ShapeDtypeStruct((seq, d_memq), BF16)],
        scratch_shapes=[pltpu.VMEM((1, d), F32), pltpu.VMEM((1, 3 * d_mix), F32)],
        compiler_params=_params("arbitrary"),
    )(*ins)


def _head_sums(xs, head0):
    out = []
    for x in xs:
        s0 = jnp.sum(jnp.where(head0, x, 0.0), axis=-1, keepdims=True)
        s1 = jnp.sum(jnp.where(head0, 0.0, x), axis=-1, keepdims=True)
        out.append(jnp.where(head0, s0, s1))
    return out


def _scan_parallel(chunks, gates, kk_w, ka_w, rk_w, consts):
    tri_incl_b, strict, incl, eye, same_head, head0 = consts
    n = SCAN_CHUNK
    heads = range(HEADS_PER_GROUP)
    head_mask = [head0, jnp.logical_not(head0)]
    nc = len(chunks)

    kk = [k * kk_w for (_, _, k, _, _) in chunks]
    kmod = [k * (1.0 + (a - 1.0) * ka_w) for (_, _, k, _, a) in chunks]
    sums = _head_sums([x * x for x in kk] + [chunks[j][0] * kmod[j] * rk_w for j in range(nc)],
                      head0)
    ss, bonus = sums[:nc], sums[nc:]
    tri2 = jnp.concatenate([tri_incl_b, tri_incl_b], axis=1)
    cum = [jnp.dot(tri2, jnp.concatenate(_split(lw), axis=0), preferred_element_type=F32)
           for (_, lw, _, _, _) in chunks]
    yield

    pre = []
    for j, (r, lw, k, v, a) in enumerate(chunks):
        kkn = kk[j] / jnp.maximum(jnp.sqrt(ss[j]), 1e-12)
        c = cum[j]
        ref = c[n // 2 - 1:n // 2, :]
        inv = jnp.exp(ref - c)
        rt = r * jnp.exp(c - ref)
        at = -kkn * jnp.exp(c - lw - ref)
        lhs = [jnp.where(m, x, 0.0).astype(BF16) for m in head_mask for x in (at, rt)]
        pre.append(dict(
            ref=ref, e_last=jnp.exp(c[n - 1:n, :] - ref), rt=rt, at=at.astype(BF16),
            vb=v.astype(BF16), lhs=jnp.concatenate(lhs, axis=0),
            bk=jnp.concatenate([(kkn * a * inv).astype(BF16), (kmod[j] * inv).astype(BF16)],
                               axis=0)))

    quad = [_bdot(pre[j]["lhs"], pre[j]["bk"], NT) for j in range(nc)]
    yield
    chains = [(j, h) for j in range(nc) for h in heads]
    strict2 = jnp.concatenate([strict, strict], axis=1)
    incl2 = jnp.concatenate([incl, incl], axis=1)
    ab_ak = {(j, h): jnp.where(strict2, quad[j][2 * h * n:(2 * h + 1) * n], 0.0)
             for j, h in chains}
    rb_rk = {(j, h): jnp.where(incl2, quad[j][(2 * h + 1) * n:(2 * h + 2) * n], 0.0).astype(BF16)
             for j, h in chains}

    nmat = {ch: ab_ak[ch][:, :n] for ch in chains}

    def lane_block(size):
        return lax.broadcasted_iota(jnp.int32, (size, n), 1) // size

    def pack_diag(mat, size, blk):
        out = mat[n - size:]
        for a in range(n // size - 2, -1, -1):
            out = jnp.where(blk == a, mat[a * size:(a + 1) * size], out)
        return out

    def block_diag(packed, size, blk):
        return jnp.concatenate([jnp.where(blk == a, packed, jnp.zeros_like(packed))
                                for a in range(n // size)], axis=0)

    size = INV_BASE
    blk = lane_block(size)
    eye_packed = pack_diag(eye, size, blk)
    packed = {ch: pack_diag(nmat[ch], size, blk) for ch in chains}
    t_p = {ch: eye_packed + packed[ch] for ch in chains}
    pb = {ch: packed[ch].astype(BF16) for ch in chains}
    power = {ch: _bdot(pb[ch], block_diag(pb[ch], size, blk)).astype(BF16) for ch in chains}
    yield
    for _ in range(int(math.log2(size)) - 2):
        both = {ch: _bdot(jnp.concatenate([t_p[ch].astype(BF16), power[ch]], axis=0),
                          block_diag(power[ch], size, blk)) for ch in chains}
        yield
        t_p = {ch: t_p[ch] + both[ch][:size] for ch in chains}
        power = {ch: both[ch][size:].astype(BF16) for ch in chains}
    t_p = {ch: t_p[ch] + _bdot(t_p[ch], block_diag(power[ch], size, blk)) for ch in chains}
    yield

    while size < n:
        odd = (blk % 2) == 1
        tp_b = {ch: t_p[ch].astype(BF16) for ch in chains}
        sub = {}
        for ch in chains:
            y = jnp.zeros((size, n), F32)
            for a in range(0, n // size, 2):
                y = jnp.where(blk == a, nmat[ch][(a + 1) * size:(a + 2) * size], y)
            sub[ch] = _bdot(y, block_diag(tp_b[ch], size, blk)).astype(BF16)
        yield
        zeros_blk = jnp.zeros((size, n), BF16)
        off = {ch: _bdot(jnp.where(odd, tp_b[ch], jnp.zeros_like(tp_b[ch])),
                         jnp.concatenate(
                             [zeros_blk if a % 2 == 0
                              else jnp.where(blk == a - 1, sub[ch], jnp.zeros_like(sub[ch]))
                              for a in range(n // size)], axis=0))
               for ch in chains}
        yield
        t_p = {ch: jnp.concatenate([jnp.where(odd, 0.0, t_p[ch]),
                                    off[ch] + jnp.where(odd, t_p[ch], 0.0)], axis=0)
               for ch in chains}
        size *= 2
        blk = lane_block(size)
    t_b = {ch: t_p[ch].astype(BF16) for ch in chains}

    def by_head(x):
        wide = (lax.broadcasted_iota(jnp.int32, x.shape, 1) % LANES) < HEAD_DIM
        return jnp.concatenate([jnp.where(wide, x, jnp.zeros_like(x)),
                                jnp.where(wide, jnp.zeros_like(x), x)], axis=0)

    def heads_on_k(mats, j):
        return jnp.concatenate([mats[(j, h)] for h in heads], axis=1)

    ak_b = {ch: ab_ak[ch][:, n:].astype(BF16) for ch in chains}
    akv = [_bdot(heads_on_k(ak_b, j), by_head(pre[j]["vb"])).astype(BF16) for j in range(nc)]
    yield
    t_x = [_bdot(heads_on_k(t_b, j), by_head(jnp.concatenate([pre[j]["at"], akv[j]], axis=1)))
           for j in range(nc)]
    yield
    ahat = [t[:, :LANES].astype(BF16) for t in t_x]
    uin = [t[:, LANES:].astype(BF16) for t in t_x]
    zero = jnp.zeros((n, LANES), BF16)
    r_x = [_bdot(heads_on_k(rb_rk, j),
                 by_head(jnp.concatenate([jnp.concatenate([ahat[j], uin[j]], axis=1),
                                          jnp.concatenate([zero, pre[j]["vb"]], axis=1)], axis=0)))
           for j in range(nc)]
    yield
    rhat = [pre[j]["rt"] + r_x[j][:, :LANES] for j in range(nc)]
    yin = [r_x[j][:, LANES:] for j in range(nc)]
    m_mat = [(eye + jnp.where(same_head, _bdot(ahat[j], pre[j]["bk"][:n], TN), 0.0))
             * pre[j]["e_last"] for j in range(nc)]
    c_mat = [jnp.where(same_head,
                       _bdot(jnp.concatenate([uin[j], pre[j]["vb"]], axis=0), pre[j]["bk"], TN),
                       0.0) * pre[j]["e_last"] for j in range(nc)]

    yield
    return [dict(rhat=rhat[j].astype(BF16), yin=yin[j], m_mat=m_mat[j].astype(BF16),
                 c_mat=c_mat[j], eref=jnp.broadcast_to(jnp.exp(pre[j]["ref"]), (SUBLANES, LANES)),
                 gate=gates[j], bvg=bonus[j] * chunks[j][3] * gates[j]) for j in range(nc)]


SCAN_SLOT_FIELDS = ("rhat", "yin", "m_mat", "c_mat", "eref", "gate", "bvg")


def _scan_sequential(slots, y_ref, state_ref, lnw, lnb, head0):
    n = SCAN_CHUNK
    rhat_s, yin_s, m_s, c_s, eref_s, gate_s, bvg_s = slots
    state = state_ref[...]
    inv_hd = 1.0 / HEAD_DIM
    for j in range(rhat_s.shape[0]):
        sp = state * eref_s[j, 0:1, :]
        y = _bdot(rhat_s[j], sp, NT) + yin_s[j]
        state = _bdot(sp, m_s[j]) + c_s[j]
        mean = _head_sums([y], head0)[0] * inv_hd
        dy = y - mean
        var = _head_sums([dy * dy], head0)[0] * inv_hd
        out = (dy * lax.rsqrt(var + LN_X_EPS) * lnw + lnb) * gate_s[j] + bvg_s[j]
        y_ref[j * n:(j + 1) * n, :] = out.astype(y_ref.dtype)
        yield
    state_ref[...] = state


def _interleave(par, seq, n_seq):
    stages = done = 0
    while True:
        try:
            next(par)
        except StopIteration as stop:
            result = stop.value
            break
        stages += 1
        while done < min(n_seq, -(-stages * n_seq // SCAN_PARALLEL_STAGES)):
            next(seq)
            done += 1
    for _ in seq:
        pass
    return result


def _rwkv_scan_kernel(r_ref, lw_ref, k_ref, v_ref, a_ref, g_ref, kk_ref, ka_ref, rk_ref,
                      lnw_ref, lnb_ref, y_ref, state_ref, *slots):
    step = pl.program_id(1)
    n_blocks = pl.num_programs(1) - 1

    @pl.when(step == 0)
    def _():
        state_ref[...] = jnp.zeros_like(state_ref)
        for slot in slots:
            slot[...] = jnp.zeros_like(slot)

    n = SCAN_CHUNK
    row = lax.broadcasted_iota(jnp.int32, (n, n), 0)
    col = lax.broadcasted_iota(jnp.int32, (n, n), 1)
    incl = row >= col
    strict = row > col
    same_head = (row // HEAD_DIM) == (col // HEAD_DIM)
    head0 = col < HEAD_DIM
    consts = (incl.astype(BF16), strict, incl, (row == col).astype(F32), same_head, head0)
    nc = r_ref.shape[0] // n

    def sequential():
        return _scan_sequential(slots, y_ref, state_ref, lnw_ref[...], lnb_ref[...], head0)

    @pl.when(step < n_blocks)
    def _():
        chunks = [tuple(ref[j * n:(j + 1) * n, :]
                        for ref in (r_ref, lw_ref, k_ref, v_ref, a_ref)) for j in range(nc)]
        gates = [g_ref[j * n:(j + 1) * n, :] for j in range(nc)]
        par = _scan_parallel(chunks, gates, kk_ref[...], ka_ref[...], rk_ref[...], consts)
        results = _interleave(par, sequential(), nc)
        for j, res in enumerate(results):
            for slot, field in zip(slots, SCAN_SLOT_FIELDS):
                slot[j] = res[field]

    @pl.when(step == n_blocks)
    def _():
        for _ in sequential():
            pass


def _rwkv_scan(r, lw, k, v, a, g, k_k, k_a, r_k, lnx_w, lnx_b):
    seq, d_mix = r.shape
    assert LANES == SCAN_CHUNK and d_mix % LANES == 0
    nc = min(SCAN_CHUNKS_PER_STEP, seq // SCAN_CHUNK)
    rows = SCAN_CHUNK * nc
    assert seq % rows == 0
    n_blocks = seq // rows
    act = pl.BlockSpec((rows, LANES), lambda p, c: (jnp.minimum(c, n_blocks - 1), p))
    out = pl.BlockSpec((rows, LANES), lambda p, c: (jnp.maximum(c - 1, 0), p))
    par = pl.BlockSpec((1, LANES), lambda p, c: (0, p))
    flat = lambda t: t.reshape(1, d_mix)
    tile = (nc, SCAN_CHUNK, LANES)
    slot_types = dict(rhat=(tile, BF16), yin=(tile, F32), m_mat=(tile, BF16), c_mat=(tile, F32),
                      eref=((nc, SUBLANES, LANES), F32), gate=(tile, F32), bvg=(tile, F32))
    return pl.pallas_call(
        _rwkv_scan_kernel,
        grid=(d_mix // LANES, n_blocks + 1),
        in_specs=[act] * 6 + [par] * 5,
        out_specs=out,
        out_shape=jax.ShapeDtypeStruct((seq, d_mix), BF16),
        scratch_shapes=([pltpu.VMEM((LANES, LANES), F32)]
                        + [pltpu.VMEM(*slot_types[f]) for f in SCAN_SLOT_FIELDS]),
        compiler_params=_params("parallel", "arbitrary"),
    )(r, lw, k, v, a, g, flat(k_k), flat(k_a), flat(r_k), flat(lnx_w), flat(lnx_b))


def _norm_proj_kernel(n_proj, out_scales, *refs):
    x_ref = refs[0]
    gains = refs[1:1 + n_proj]
    weights = refs[1 + n_proj:1 + 2 * n_proj]
    outs = refs[1 + 2 * n_proj:]
    xn = _rms_scale(x_ref[...])
    o = 0
    for g_ref, w_ref in zip(gains, weights):
        y = _bdot(xn * g_ref[...], w_ref[...])
        start = 0
        while start < y.shape[1]:
            width = outs[o].shape[1]
            part = y[:, start:start + width]
            if out_scales[o] != 1.0:
                part = part * out_scales[o]
            outs[o][...] = part.astype(outs[o].dtype)
            start += width
            o += 1


def _norm_proj(x, projections, out_scales):
    seq, d = x.shape
    tm = min(ROW_TILE, seq)
    row = lambda n: pl.BlockSpec((tm, n), lambda i: (i, 0))
    gains = [g.reshape(1, d) for g, _, _ in projections]
    weights = [w.astype(BF16) for _, w, _ in projections]
    widths = [n for _, _, ws in projections for n in ws]
    assert len(out_scales) == len(widths)
    return pl.pallas_call(
        functools.partial(_norm_proj_kernel, len(projections), tuple(out_scales)),
        grid=(seq // tm,),
        in_specs=([row(d)] + [_const_spec((1, d))] * len(gains)
                  + [_const_spec(w.shape) for w in weights]),
        out_specs=[row(n) for n in widths],
        out_shape=[jax.ShapeDtypeStruct((seq, n), BF16) for n in widths],
        compiler_params=_params("arbitrary"),
    )(x, *gains, *weights)


def _band_bias(table):
    n_heads, n_tab = table.shape
    width = BAND + CHUNK - 1
    ext = jnp.concatenate(
        [table, jnp.broadcast_to(table[:, -1:], (n_heads, width - n_tab))], axis=1).astype(F32)
    rep = jnp.tile(ext, (1, CHUNK + 1))[:, :CHUNK * (width + 1)]
    windows = rep.reshape(n_heads, CHUNK, width + 1)[:, :, :BAND]
    return windows[:, :, ::-1]


def _chunk_attn_tile(first_tile, q_ref, kp_ref, kc_ref, vp_ref, vc_ref, bias_ref, o_ref):
    tq = q_ref.shape[0]
    k_win = jnp.concatenate([kp_ref[...], kc_ref[...]], axis=0)
    v_win = jnp.concatenate([vp_ref[...], vc_ref[...]], axis=0)
    bias = jnp.concatenate([bias_ref[hd] for hd in range(HEADS_PER_GROUP)], axis=0)
    lane = lax.broadcasted_iota(jnp.int32, (CHUNK, LANES), 1)
    col = lax.broadcasted_iota(jnp.int32, (HEADS_PER_GROUP * CHUNK, BAND), 1)
    blocks = range(tq // CHUNK)
    scores = []
    for c in blocks:
        q = q_ref[c * CHUNK:(c + 1) * CHUNK, :]
        q2 = jnp.concatenate(
            [jnp.where((lane // HEAD_DIM) == hd, q, jnp.zeros_like(q))
             for hd in range(HEADS_PER_GROUP)], axis=0)
        scores.append(_bdot(q2, k_win[c * CHUNK:c * CHUNK + BAND, :], NT))
    probs = []
    for c in blocks:
        s = scores[c] + bias
        if first_tile:
            s = jnp.where(col >= tq - c * CHUNK, s, MASK_VALUE)
        probs.append(jnp.exp2(s - jnp.max(s, axis=-1, keepdims=True)).astype(BF16))
    ones = jnp.ones((BAND, LANES), BF16)
    for c in blocks:
        both = _bdot(probs[c], jnp.concatenate([v_win[c * CHUNK:c * CHUNK + BAND, :], ones],
                                               axis=1))
        o2 = both[:, :LANES] / both[:, LANES:]
        out = o2[:CHUNK, :]
        for hd in range(1, HEADS_PER_GROUP):
            out = jnp.where((lane // HEAD_DIM) == hd, o2[hd * CHUNK:(hd + 1) * CHUNK, :], out)
        o_ref[c * CHUNK:(c + 1) * CHUNK, :] = out.astype(o_ref.dtype)


def _chunk_attn_kernel(*refs):
    is_first = pl.program_id(1) == 0
    pl.when(is_first)(functools.partial(_chunk_attn_tile, True, *refs))
    pl.when(jnp.logical_not(is_first))(functools.partial(_chunk_attn_tile, False, *refs))


def _chunk_attn(q, k, v, rel_table):
    seq, d_mix = q.shape
    tq = ATTN_TILE
    assert seq % tq == 0
    cur = pl.BlockSpec((tq, LANES), lambda p, i: (i, p))
    prev = pl.BlockSpec((tq, LANES), lambda p, i: (jnp.maximum(i - 1, 0), p))
    bias = _band_bias(rel_table) * LOG2_E
    return pl.pallas_call(
        _chunk_attn_kernel,
        grid=(d_mix // LANES, seq // tq),
        in_specs=[cur, prev, cur, prev, cur,
                  pl.BlockSpec((HEADS_PER_GROUP, CHUNK, BAND), lambda p, i: (p, 0, 0))],
        out_specs=cur,
        out_shape=jax.ShapeDtypeStruct((seq, d_mix), BF16),
        compiler_params=_params("parallel", "arbitrary"),
    )(q, k, k, v, v, bias)


def _layer_post_kernel(has_final, d_mix, d_ff, *refs):
    refs = list(refs)
    (x_ref, mix_ref, qm_ref, kvm_ref, wout_ref, ln2_ref, fin_ref, cw_ref, cb_ref,
     fout_ref) = refs[:10]
    refs = refs[10:]
    lnf_ref = refs.pop(0) if has_final else None
    out_ref, carry = refs

    @pl.when(pl.program_id(0) == 0)
    def _():
        carry[...] = jnp.zeros_like(carry)

    tm = x_ref.shape[0]
    d_memq = qm_ref.shape[1]
    mix = mix_ref[...]

    qm = qm_ref[...] * HEAD_DIM ** -0.5
    km = kvm_ref[:, :d_memq]
    vm = kvm_ref[:, d_memq:]
    lane = lax.broadcasted_iota(jnp.int32, qm.shape, 1)
    head_masks = [(lane // HEAD_DIM) == hd for hd in range(d_memq // HEAD_DIM)]
    scores = [_bdot(jnp.where(mh, qm, jnp.zeros_like(qm)), km, NT) for mh in head_masks]
    proj_mix = _bdot(mix, wout_ref[:d_mix, :])
    exps = [jnp.exp(s - jnp.max(s, axis=-1, keepdims=True)) for s in scores]
    heads_out = [_bdot(e, vm) / jnp.sum(e, axis=-1, keepdims=True) for e in exps]
    mo = heads_out[0]
    for mh, o_h in zip(head_masks[1:], heads_out[1:]):
        mo = jnp.where(mh, o_h, mo)
    x1 = x_ref[...] + (proj_mix + _bdot(mo, wout_ref[d_mix:, :]))

    h2 = (_rms_scale(x1) * ln2_ref[...]).astype(BF16)

    def in_dots(c0):
        return (jnp.dot(h2, fin_ref[:, c0:c0 + FFN_COLS], preferred_element_type=F32),
                jnp.dot(h2, fin_ref[:, d_ff + c0:d_ff + c0 + FFN_COLS],
                        preferred_element_type=F32))

    slabs = list(range(0, d_ff, FFN_COLS))
    ffn = None
    nxt = in_dots(slabs[0])
    for idx, c0 in enumerate(slabs):
        cols = slice(c0, c0 + FFN_COLS)
        gate, val = nxt
        if idx + 1 < len(slabs):
            nxt = in_dots(slabs[idx + 1])
        prev = carry[SUBLANES - (CONV_W - 1):, cols]
        carry[:, cols] = gate[tm - SUBLANES:, :]
        conv = cb_ref[:, cols] + cw_ref[CONV_W - 1:CONV_W, cols] * gate
        for j in range(1, CONV_W):
            conv = conv + (cw_ref[CONV_W - 1 - j:CONV_W - j, cols]
                           * _shift_rows(gate, prev[CONV_W - 1 - j:, :], j))
        act = 0.5 * conv * (1.0 + lax.erf(conv * (2.0 ** -0.5))) * val
        slab = _bdot(act, fout_ref[cols, :])
        ffn = slab if ffn is None else ffn + slab
    acc = x1 + ffn
    if has_final:
        acc = _rms_scale(acc) * lnf_ref[...]
    out_ref[...] = acc


def _layer_post(x, mix, qm, layer, kv_mem, w_out, ln2, ffn_in, conv_w, conv_b, ffn_out, ln_f):
    seq, d = x.shape
    d_mix = mix.shape[1]
    d_memq = qm.shape[1]
    d_ff = ffn_out.shape[1]
    assert d_ff % FFN_COLS == 0 and conv_w.shape[0] == CONV_W
    tm = min(POST_TILE, seq)
    row = lambda n: pl.BlockSpec((tm, n), lambda i: (i, 0))
    ins, specs = [x, mix], [row(d), row(d_mix)]
    ins += [qm, kv_mem, w_out, ln2.reshape(1, d), ffn_in, conv_w, conv_b.reshape(1, d_ff),
            ffn_out]
    specs += [row(d_memq), _layer_spec(kv_mem.shape, layer), _layer_spec(w_out.shape, layer),
              _const_spec((1, d)), _layer_spec(ffn_in.shape, layer), _const_spec(conv_w.shape),
              _const_spec((1, d_ff)), _layer_spec(ffn_out.shape, layer)]
    if ln_f is not None:
        ins.append(ln_f.reshape(1, d))
        specs.append(_const_spec((1, d)))
    return pl.pallas_call(
        functools.partial(_layer_post_kernel, ln_f is not None, d_mix, d_ff),
        grid=(seq // tm,),
        in_specs=specs,
        out_specs=row(d),
        out_shape=jax.ShapeDtypeStruct((seq, d), F32),
        scratch_shapes=[pltpu.VMEM((SUBLANES, d_ff), F32)],
        compiler_params=_params("arbitrary"),
    )(*ins)


def kernel(x, mem, mem_norm, ln1, ln2, w_out, w_mem_kv, ffn_in, ffn_conv, ffn_conv_b, ffn_out, a_w_in, a_mu_rkv, a_mu_x, a_w0, a_w1, a_w2, a_a0, a_a1, a_a2, a_g1, a_g2, a_k_k, a_k_a, a_r_k, a_lnx_w, a_lnx_b, a_mu_v, a_v0, a_v1, a_v2, ln_kv, w_kv, b_w_in, b_rel, ln_f):
    bsz, seq, d = x.shape
    assert bsz == 1 and mem.shape[0] == 1
    depth = ln1.shape[0]
    n_a = a_w_in.shape[0]
    d_mix = a_w0.shape[1]
    xs = x.reshape(seq, d)
    kv_mem = _mem_kv(mem.reshape(mem.shape[1], d), mem_norm, w_mem_kv)
    a_w_in_b, w_out_b = a_w_in.astype(BF16), w_out.astype(BF16)
    ffn_in_b, ffn_out_b = ffn_in.astype(BF16), ffn_out.astype(BF16)

    v_first = None
    k_s = v_s = None
    for layer in range(depth):
        last = ln_f if layer == depth - 1 else None
        if layer < n_a:
            i = layer
            vres = None if i == 0 else (a_mu_v[i - 1], a_v0[i - 1], a_v1[i - 1], a_v2[i - 1])
            r, lw, k, v, a, g, qm = _rwkv_pre(
                xs, ln1[layer], a_w_in_b, i, a_mu_rkv[i], a_mu_x[i], a_w0[i], a_w1[i], a_w2[i],
                a_a0[i], a_a1[i], a_a2[i], a_g1[i], a_g2[i], vres, v_first)
            if i == 0:
                v_first = v
            mix = _rwkv_scan(r, lw, k, v, a, g, a_k_k[i], a_k_a[i], a_r_k[i],
                             a_lnx_w[i], a_lnx_b[i])
        else:
            j = layer - n_a
            d_memq = b_w_in.shape[2] - d_mix
            projections = [(ln1[layer], b_w_in[j], [d_mix, d_memq])]
            if j == 0:
                projections.append((ln_kv, w_kv, [d_mix, d_mix]))
            scales = [ATTN_Q_SCALE] + [1.0] * (1 + 2 * (len(projections) - 1))
            outs = _norm_proj(xs, projections, scales)
            q, qm = outs[:2]
            if j == 0:
                k_s, v_s = outs[2:]
            mix = _chunk_attn(q, k_s, v_s, b_rel[j])
        xs = _layer_post(xs, mix, qm, layer, kv_mem, w_out_b, ln2[layer], ffn_in_b,
                         ffn_conv[layer], ffn_conv_b[layer], ffn_out_b, last)
    return xs.reshape(bsz, seq, d)
```
